```python
import math
import jax, jax.numpy as jnp
from jax import lax
import numpy as np

D_MODEL = 2048
BATCH = 2
SEQ = 8192
DEPTH = 2

MEM_LEN = 256
EPS = 1e-6
DA_HEADS = 8
DA_HEAD_DIM = 64
DA_WIDTH = DA_HEADS * 2 * DA_HEAD_DIM
Q_BLOCK = 128
POOL_WINDOWS = (2, 4, 8, 16)
POOL_WIDTH = D_MODEL // 2
POOL_GROUP = POOL_WIDTH // len(POOL_WINDOWS)
LRU_WIDTH = D_MODEL // 2
LRU_BLOCKS = 8
LRU_BLOCK = LRU_WIDTH // LRU_BLOCKS
CONV_WIDTH = 4
LRU_C = 8.0
MEM_HEADS = 4
MEM_HEAD_DIM = 128
MEM_WIDTH = MEM_HEADS * MEM_HEAD_DIM
N_BRANCH = 4
BRANCH_SIZES = (DA_WIDTH, POOL_WIDTH, LRU_WIDTH, MEM_WIDTH)
BRANCH_OFFSETS = (0, DA_WIDTH, DA_WIDTH + POOL_WIDTH, DA_WIDTH + POOL_WIDTH + LRU_WIDTH, DA_WIDTH + POOL_WIDTH + LRU_WIDTH + MEM_WIDTH)
BRANCH_WIDTH = BRANCH_OFFSETS[-1]
SPLIT_SIZES = (DA_WIDTH, DA_WIDTH, DA_WIDTH, POOL_WIDTH, LRU_WIDTH, MEM_WIDTH, BRANCH_WIDTH, N_BRANCH * D_MODEL)
SPLIT_POINTS = tuple(int(v) for v in np.cumsum(SPLIT_SIZES)[:-1])
N_IN = int(sum(SPLIT_SIZES))

kernel_name = "gated_parallel_hybrid_diffattn_pool_rglru_mem"


def rms_norm(x, g):
    xf = x.astype(jnp.float32)
    y = xf * lax.rsqrt(jnp.mean(xf * xf, axis=-1, keepdims=True) + EPS)
    return (y * g.astype(jnp.float32)).astype(x.dtype)


def block_diag_linear(x, w):
    b, s, _ = x.shape
    g, n, _ = w.shape
    return jnp.einsum('bsgi,gij->bsgj', x.reshape(b, s, g, n), w).reshape(b, s, g * n)


def diff_attention(q, k, v, lam_vecs, subln_g, lambda_init):
    b, s, _ = q.shape
    q = q.reshape(b, s, DA_HEADS, 2, DA_HEAD_DIM)
    k = k.reshape(b, s, DA_HEADS, 2, DA_HEAD_DIM)
    v = v.reshape(b, s, DA_HEADS, 2 * DA_HEAD_DIM)
    lf = lam_vecs.astype(jnp.float32)
    lam = jnp.exp(jnp.sum(lf[0] * lf[1])) - jnp.exp(jnp.sum(lf[2] * lf[3])) + lambda_init
    n_blk = s // Q_BLOCK
    q_blocks = q.reshape(b, n_blk, Q_BLOCK, DA_HEADS, 2, DA_HEAD_DIM).transpose(1, 0, 2, 3, 4, 5)
    k_pos = jnp.arange(s)
    scale = DA_HEAD_DIM ** -0.5

    def one_block(args):
        qb, blk = args
        q_pos = blk * Q_BLOCK + jnp.arange(Q_BLOCK)
        sc = jnp.einsum('bqhcd,bkhcd->bhcqk', qb, k).astype(jnp.float32) * scale
        causal = k_pos[None, :] <= q_pos[:, None]
        sc = jnp.where(causal, sc, -jnp.inf)
        p = jax.nn.softmax(sc, axis=-1)
        p_diff = (p[:, :, 0] - lam * p[:, :, 1]).astype(v.dtype)
        return jnp.einsum('bhqk,bkhe->bqhe', p_diff, v)

    o = lax.map(one_block, (q_blocks, jnp.arange(n_blk)))
    o = o.transpose(1, 0, 2, 3, 4).reshape(b, s, DA_HEADS, 2 * DA_HEAD_DIM)
    o = rms_norm(o, subln_g) * (1.0 - lambda_init)
    return o.reshape(b, s, DA_WIDTH)


def multiscale_pool(x, w_grp, scale):
    b, s, _ = x.shape
    xg = x.reshape(b, s, len(POOL_WINDOWS), POOL_GROUP)
    t = jnp.arange(s)
    outs = []
    for gi, w in enumerate(POOL_WINDOWS):
        xi = xg[:, :, gi].astype(jnp.float32)
        c = jnp.pad(jnp.cumsum(xi, axis=1), ((0, 0), (1, 0), (0, 0)))
        upper = c[:, 1:]
        lower = jnp.pad(c[:, :s - w + 1], ((0, 0), (w - 1, 0), (0, 0)))
        count = jnp.minimum(t + 1, w).astype(jnp.float32)
        mean = (upper - lower) / count[None, :, None]
        outs.append((mean - xi).astype(x.dtype))
    d = jnp.stack(outs, axis=2)
    y = jnp.einsum('bsgi,gij->bsgj', d, w_grp).reshape(b, s, POOL_WIDTH)
    return y * scale


def rg_lru_branch(x, conv_w, conv_b, w_a, b_a, w_x, b_x, lam):
    b, s, _ = x.shape
    xc = lax.conv_general_dilated(x, conv_w[:, None, :], window_strides=(1,),
                                  padding=[(CONV_WIDTH - 1, 0)],
                                  dimension_numbers=('NWC', 'WIO', 'NWC'),
                                  feature_group_count=LRU_WIDTH) + conv_b
    r = jax.nn.sigmoid(block_diag_linear(xc, w_a) + b_a).astype(jnp.float32)
    i = jax.nn.sigmoid(block_diag_linear(xc, w_x) + b_x)
    log_a = -LRU_C * r * jax.nn.softplus(-lam.astype(jnp.float32))
    a = jnp.exp(log_a)
    mult = jnp.sqrt(-jnp.expm1(2.0 * log_a))
    mult = jnp.where((jnp.arange(s) == 0)[None, :, None], 1.0, mult)
    u = mult * (i * xc).astype(jnp.float32)

    def combine(left, right):
        a_l, b_l = left
        a_r, b_r = right
        return a_l * a_r, a_r * b_l + b_r

    _, h = lax.associative_scan(combine, (a, u), axis=1)
    return h.astype(x.dtype)


def memory_attention(q, mem_k, mem_v):
    b, s, _ = q.shape
    q = q.reshape(b, s, MEM_HEADS, MEM_HEAD_DIM)
    sc = jnp.einsum('bshd,bmhd->bhsm', q, mem_k).astype(jnp.float32) * (MEM_HEAD_DIM ** -0.5)
    p = jax.nn.softmax(sc, axis=-1).astype(mem_v.dtype)
    return jnp.einsum('bhsm,bmhd->bshd', p, mem_v).reshape(b, s, MEM_WIDTH)


def hybrid_layer(x, mem, layer_idx, norm_g, w_in, lam_vecs, subln_g, pool_w, pool_scale,
                 conv_w, conv_b, lru_wa, lru_ba, lru_wx, lru_bx, lru_lambda,
                 mem_norm_g, w_mem_kv, w_branch, w_out):
    b, s, _ = x.shape
    h = rms_norm(x, norm_g)
    proj = h @ w_in
    q, k, v, x_pool, x_lru, q_mem, gate_in, merge_in = jnp.split(proj, SPLIT_POINTS, axis=-1)
    lambda_init = 0.8 - 0.6 * math.exp(-0.3 * layer_idx)
    y_a = diff_attention(q, k, v, lam_vecs, subln_g, lambda_init)
    y_b = multiscale_pool(x_pool, pool_w, pool_scale)
    y_c = rg_lru_branch(x_lru, conv_w, conv_b, lru_wa, lru_ba, lru_wx, lru_bx, lru_lambda)
    mkv = rms_norm(mem, mem_norm_g) @ w_mem_kv
    mk, mv = jnp.split(mkv, 2, axis=-1)
    mk = mk.reshape(b, MEM_LEN, MEM_HEADS, MEM_HEAD_DIM)
    mv = mv.reshape(b, MEM_LEN, MEM_HEADS, MEM_HEAD_DIM)
    y_m = memory_attention(q_mem, mk, mv)
    branches = jnp.concatenate([y_a, y_b, y_c, y_m], axis=-1) * jax.nn.silu(gate_in)
    merge_gates = jax.nn.sigmoid(merge_in).reshape(b, s, N_BRANCH, D_MODEL)
    merged = jnp.zeros_like(x)
    for bi in range(N_BRANCH):
        st, en = BRANCH_OFFSETS[bi], BRANCH_OFFSETS[bi + 1]
        merged = merged + merge_gates[:, :, bi] * (branches[..., st:en] @ w_branch[st:en])
    return x + merged @ w_out


def setup_inputs(seed: int = 0) -> dict:
    key = jax.random.key(seed)
    ks = jax.random.split(key, 24)
    f32 = jnp.float32
    nrm = lambda k, shape, sc: jax.random.normal(k, shape, f32) * sc
    x = jax.random.normal(ks[0], (BATCH, SEQ, D_MODEL), f32)
    mem = jax.random.normal(ks[1], (BATCH, MEM_LEN, D_MODEL), f32)
    norm_g = 1.0 + nrm(ks[2], (DEPTH, D_MODEL), 0.02)
    w_in = nrm(ks[3], (DEPTH, D_MODEL, N_IN), D_MODEL ** -0.5)
    lam_vecs = nrm(ks[4], (DEPTH, 4, DA_HEAD_DIM), 0.1)
    subln_g = 1.0 + nrm(ks[5], (DEPTH, 2 * DA_HEAD_DIM), 0.02)
    pool_w = nrm(ks[6], (DEPTH, len(POOL_WINDOWS), POOL_GROUP, POOL_GROUP), POOL_GROUP ** -0.5)
    pool_scale = 1.0 + nrm(ks[7], (DEPTH, POOL_WIDTH), 0.02)
    conv_w = nrm(ks[8], (DEPTH, CONV_WIDTH, LRU_WIDTH), CONV_WIDTH ** -0.5)
    conv_b = nrm(ks[9], (DEPTH, LRU_WIDTH), 0.01)
    lru_wa = nrm(ks[10], (DEPTH, LRU_BLOCKS, LRU_BLOCK, LRU_BLOCK), LRU_BLOCK ** -0.5)
    lru_ba = nrm(ks[11], (DEPTH, LRU_WIDTH), 0.01)
    lru_wx = nrm(ks[12], (DEPTH, LRU_BLOCKS, LRU_BLOCK, LRU_BLOCK), LRU_BLOCK ** -0.5)
    lru_bx = nrm(ks[13], (DEPTH, LRU_WIDTH), 0.01)
    a_pow_c = jax.random.uniform(ks[14], (DEPTH, LRU_WIDTH), f32, 0.9, 0.999)
    a0 = a_pow_c ** (1.0 / LRU_C)
    lru_lambda = jnp.log(a0) - jnp.log1p(-a0)
    mem_norm_g = 1.0 + nrm(ks[15], (DEPTH, D_MODEL), 0.02)
    w_mem_kv = nrm(ks[16], (DEPTH, D_MODEL, 2 * MEM_WIDTH), D_MODEL ** -0.5)
    bkeys = jax.random.split(ks[17], N_BRANCH)
    w_branch = jnp.concatenate([nrm(bkeys[i], (DEPTH, BRANCH_SIZES[i], D_MODEL), BRANCH_SIZES[i] ** -0.5)
                                for i in range(N_BRANCH)], axis=1)
    w_out = nrm(ks[18], (DEPTH, D_MODEL, D_MODEL), D_MODEL ** -0.5)
    final_g = 1.0 + nrm(ks[19], (D_MODEL,), 0.02)
    return {"x": x, "mem": mem, "norm_g": norm_g, "w_in": w_in, "lam_vecs": lam_vecs,
            "subln_g": subln_g, "pool_w": pool_w, "pool_scale": pool_scale,
            "conv_w": conv_w, "conv_b": conv_b, "lru_wa": lru_wa, "lru_ba": lru_ba,
            "lru_wx": lru_wx, "lru_bx": lru_bx, "lru_lambda": lru_lambda,
            "mem_norm_g": mem_norm_g, "w_mem_kv": w_mem_kv, "w_branch": w_branch,
            "w_out": w_out, "final_g": final_g}


def reference(x, mem, norm_g, w_in, lam_vecs, subln_g, pool_w, pool_scale, conv_w, conv_b,
              lru_wa, lru_ba, lru_wx, lru_bx, lru_lambda, mem_norm_g, w_mem_kv, w_branch,
              w_out, final_g):
    for l in range(DEPTH):
        x = hybrid_layer(x, mem, l, norm_g[l], w_in[l], lam_vecs[l], subln_g[l], pool_w[l],
                         pool_scale[l], conv_w[l], conv_b[l], lru_wa[l], lru_ba[l], lru_wx[l],
                         lru_bx[l], lru_lambda[l], mem_norm_g[l], w_mem_kv[l], w_branch[l], w_out[l])
    return rms_norm(x, final_g)
```

```python
import functools
import math

import jax
import jax.numpy as jnp
from jax import lax
from jax.experimental import pallas as pl
from jax.experimental.pallas import tpu as pltpu

F32 = jnp.float32
BF16 = jnp.bfloat16

EPS = 1e-6
DA_HEADS = 8
DA_HEAD_DIM = 64
DA_V_DIM = 2 * DA_HEAD_DIM
POOL_WINDOWS = (2, 4, 8, 16)
POOL_HALO = 16
LRU_BLOCKS = 8
CONV_WIDTH = 4
CONV_HALO = 8
LRU_C = 8.0
MEM_HEADS = 4
MEM_HEAD_DIM = 128
LANES = 128
SUBLANES = 8
V7X_VMEM_BYTES = 64 * 1024 * 1024
MASK_VALUE = -1e30
LOG2E = 1.4426950408889634


def _compiler_params(semantics, vmem_bytes):
    assert vmem_bytes < V7X_VMEM_BYTES
    return pltpu.CompilerParams(dimension_semantics=semantics, vmem_limit_bytes=int(vmem_bytes))


def _col_block(col, width):
    assert col % width == 0, (col, width)
    return col // width


def _rms(x, g):
    ms = jnp.mean(x * x, axis=-1, keepdims=True)
    return x * lax.rsqrt(ms + EPS) * g


def _silu(x):
    return x * jax.nn.sigmoid(x)


def _norm_proj_kernel(x_ref, g_ref, w_ref, o_ref, h_ref):
    @pl.when(pl.program_id(1) == 0)
    def _():
        h_ref[...] = _rms(x_ref[...], g_ref[...]).astype(BF16)

    o_ref[...] = jnp.dot(h_ref[...], w_ref[...], preferred_element_type=F32).astype(o_ref.dtype)


def _norm_proj(x, g, w, *, tm, tn, name):
    n, d = x.shape
    nc = w.shape[1]
    vmem = 2 * (tm * d * 4 + d * tn * 2 + tm * tn * 2) + tm * d * 2 + (4 << 20)
    return pl.pallas_call(
        _norm_proj_kernel,
        grid=(n // tm, nc // tn),
        in_specs=[
            pl.BlockSpec((tm, d), lambda i, j: (i, 0)),
            pl.BlockSpec((1, d), lambda i, j: (0, 0)),
            pl.BlockSpec((d, tn), lambda i, j: (0, j)),
        ],
        out_specs=pl.BlockSpec((tm, tn), lambda i, j: (i, j)),
        out_shape=jax.ShapeDtypeStruct((n, nc), BF16),
        scratch_shapes=[pltpu.VMEM((tm, d), BF16)],
        compiler_params=_compiler_params(("parallel", "arbitrary"), vmem),
        name=name,
    )(x, g.reshape(1, d), w)


def _diff_attn_kernel(q_ref, k_ref, v_ref, gate_ref, lamv_ref, subg_ref, o_ref,
                      qs_ref, m_ref, l_ref, acc_ref, *, tq, lambda_init):
    qi = pl.program_id(2)
    rows = 2 * tq
    scale2 = (DA_HEAD_DIM ** -0.5) * LOG2E

    q = q_ref[...]
    lane = lax.broadcasted_iota(jnp.int32, q.shape, 1)
    zero = jnp.zeros_like(q)
    qs_ref[0:tq, :] = jnp.where(lane < DA_HEAD_DIM, q, zero)
    qs_ref[tq:rows, :] = jnp.where(lane >= DA_HEAD_DIM, q, zero)
    m_ref[...] = jnp.full(m_ref.shape, MASK_VALUE, F32)
    l_ref[...] = jnp.zeros(l_ref.shape, F32)
    acc_ref[...] = jnp.zeros(acc_ref.shape, F32)

    def step(j, masked):
        start = pl.multiple_of(j * tq, tq)
        k = k_ref[pl.ds(start, tq), :]
        v = v_ref[pl.ds(start, tq), :]
        s = lax.dot_general(qs_ref[...], k, (((1,), (1,)), ((), ())),
                            preferred_element_type=F32)
        if masked:
            row = lax.broadcasted_iota(jnp.int32, s.shape, 0)
            col = lax.broadcasted_iota(jnp.int32, s.shape, 1)
            q_pos = jnp.where(row >= tq, row - tq, row)
            s = jnp.where(col <= q_pos, s, MASK_VALUE)
        m_prev = m_ref[...]
        m_next = jnp.maximum(m_prev, jnp.max(s, axis=-1, keepdims=True))
        p = jnp.exp2((s - pltpu.repeat(m_next, tq // LANES, 1)) * scale2)
        alpha = jnp.exp2((m_prev - m_next) * scale2)
        l_ref[...] = alpha * l_ref[...] + jnp.sum(p, axis=-1, keepdims=True)
        acc_ref[...] = alpha * acc_ref[...] + jnp.dot(p.astype(BF16), v, preferred_element_type=F32)
        m_ref[...] = m_next

    def unmasked_body(j, carry):
        step(j, False)
        return carry

    lax.fori_loop(0, qi, unmasked_body, 0)
    step(qi, True)

    o_maps = acc_ref[...] / l_ref[...]
    lv = lamv_ref[...]
    lam = (jnp.exp(jnp.sum(lv[0:1, :] * lv[1:2, :], axis=-1, keepdims=True))
           - jnp.exp(jnp.sum(lv[2:3, :] * lv[3:4, :], axis=-1, keepdims=True)) + lambda_init)
    o = o_maps[0:tq, :] - lam * o_maps[tq:rows, :]
    y = _rms(o, subg_ref[...]) * (1.0 - lambda_init)
    o_ref[...] = (y * _silu(gate_ref[...].astype(F32))).astype(o_ref.dtype)


def _diff_attention(proj, lam_vecs, subln_g, *, batch, seq, col_q, col_k, col_v, col_gate, lambda_init, tq):
    n = batch * seq
    nq = seq // tq
    hw = DA_V_DIM
    bq, bk, bv, bg = (_col_block(c, hw) for c in (col_q, col_k, col_v, col_gate))
    vmem = (2 * (2 * tq * hw * 2 + 2 * seq * hw * 2 + tq * hw * 2) + 2 * tq * hw * (2 + 12)
            + 6 * 2 * tq * tq * 4 + (4 << 20))
    kernel = functools.partial(_diff_attn_kernel, tq=tq, lambda_init=lambda_init)
    return pl.pallas_call(
        kernel,
        grid=(batch, DA_HEADS, nq),
        in_specs=[
            pl.BlockSpec((tq, hw), lambda b, h, i: (b * nq + i, bq + h)),
            pl.BlockSpec((seq, hw), lambda b, h, i: (b, bk + h)),
            pl.BlockSpec((seq, hw), lambda b, h, i: (b, bv + h)),
            pl.BlockSpec((tq, hw), lambda b, h, i: (b * nq + i, bg + h)),
            pl.BlockSpec(lam_vecs.shape, lambda b, h, i: (0, 0)),
            pl.BlockSpec((1, hw), lambda b, h, i: (0, 0)),
        ],
        out_specs=pl.BlockSpec((tq, hw), lambda b, h, i: (b * nq + i, h)),
        out_shape=jax.ShapeDtypeStruct((n, DA_HEADS * hw), BF16),
        scratch_shapes=[
            pltpu.VMEM((2 * tq, hw), BF16),
            pltpu.VMEM((2 * tq, LANES), F32),
            pltpu.VMEM((2 * tq, LANES), F32),
            pltpu.VMEM((2 * tq, hw), F32),
        ],
        compiler_params=_compiler_params(("parallel", "parallel", "arbitrary"), vmem),
        name="diff_attention",
    )(proj, proj, proj, proj, lam_vecs, subln_g.reshape(1, hw))


def _gate_cols(gate_refs, cols):
    piece = gate_refs[0].shape[1]
    idx, start = divmod(cols.start, piece)
    assert cols.stop - cols.start <= piece - start
    return _silu(gate_refs[idx][:, start:start + cols.stop - cols.start].astype(F32))


def _pool_kernel(x_ref, halo_ref, gate_lo_ref, gate_hi_ref, w_ref, scale_ref, o_ref, xe_ref, *, ts, tiles_per_seq):
    gate_refs = (gate_lo_ref, gate_hi_ref)
    t = pl.program_id(0) % tiles_per_seq
    halo = halo_ref[...].astype(F32)
    xe_ref[0:POOL_HALO, :] = jnp.where(t == 0, jnp.zeros_like(halo), halo)
    x = x_ref[...].astype(F32)
    xe_ref[POOL_HALO:POOL_HALO + ts, :] = x
    gw = x.shape[1] // len(POOL_WINDOWS)
    pos1 = t * ts + lax.broadcasted_iota(jnp.int32, (ts, gw), 0) + 1
    for gi, w in enumerate(POOL_WINDOWS):
        cols = slice(gi * gw, (gi + 1) * gw)
        xg = x[:, cols]
        wsum = xg
        for back in range(1, w):
            wsum = wsum + xe_ref[POOL_HALO - back:POOL_HALO - back + ts, cols]
        count = jnp.minimum(pos1, w).astype(F32)
        d = (wsum / count - xg).astype(BF16)
        y = jnp.dot(d, w_ref[gi], preferred_element_type=F32) * scale_ref[:, cols]
        o_ref[:, cols] = (y * _gate_cols(gate_refs, cols)).astype(o_ref.dtype)


def _multiscale_pool(proj, pool_w, pool_scale, *, seq, col_x, col_gate, ts):
    n = proj.shape[0]
    width = pool_scale.shape[-1]
    tiles_per_seq = seq // ts
    halo_per_tile = ts // POOL_HALO
    half = width // 2
    bx, bg = _col_block(col_x, width), _col_block(col_gate, half)
    vmem = 2 * (3 * ts * width * 2 + POOL_HALO * width * 2 + pool_w.size * 2) + 8 * ts * width * 4 + (4 << 20)
    kernel = functools.partial(_pool_kernel, ts=ts, tiles_per_seq=tiles_per_seq)
    return pl.pallas_call(
        kernel,
        grid=(n // ts,),
        in_specs=[
            pl.BlockSpec((ts, width), lambda i: (i, bx)),
            pl.BlockSpec((POOL_HALO, width), lambda i: (jnp.maximum(i * halo_per_tile - 1, 0), bx)),
            pl.BlockSpec((ts, half), lambda i: (i, bg)),
            pl.BlockSpec((ts, half), lambda i: (i, bg + 1)),
            pl.BlockSpec(pool_w.shape, lambda i: (0, 0, 0)),
            pl.BlockSpec((1, width), lambda i: (0, 0)),
        ],
        out_specs=pl.BlockSpec((ts, width), lambda i: (i, 0)),
        out_shape=jax.ShapeDtypeStruct((n, width), BF16),
        scratch_shapes=[pltpu.VMEM((POOL_HALO + ts, width), F32)],
        compiler_params=_compiler_params(("parallel",), vmem),
        name="multiscale_pool",
    )(proj, proj, proj, proj, pool_w, pool_scale.reshape(1, width))


def _lru_kernel(x_ref, gate_lo_ref, gate_hi_ref, cw_ref, cb_ref, wax_ref, ba_ref, bx_ref, lam_ref, o_ref,
                xe_ref, a_ref, u_ref, carry_ref, *, ts):
    t = pl.program_id(1)
    width = x_ref.shape[1]
    bw = width // LRU_BLOCKS

    @pl.when(t == 0)
    def _():
        xe_ref[0:CONV_HALO, :] = jnp.zeros((CONV_HALO, width), F32)
        carry_ref[...] = jnp.zeros(carry_ref.shape, F32)

    @pl.when(t > 0)
    def _():
        xe_ref[0:CONV_HALO, :] = xe_ref[ts:ts + CONV_HALO, :]

    xe_ref[CONV_HALO:CONV_HALO + ts, :] = x_ref[...].astype(F32)

    xc = cb_ref[...] + cw_ref[CONV_WIDTH - 1:CONV_WIDTH, :] * xe_ref[CONV_HALO:CONV_HALO + ts, :]
    for back in range(1, CONV_WIDTH):
        tap = CONV_WIDTH - 1 - back
        xc = xc + cw_ref[tap:tap + 1, :] * xe_ref[CONV_HALO - back:CONV_HALO - back + ts, :]
    xcb = xc.astype(BF16)

    neg_lam = -lam_ref[...]
    softplus = jnp.maximum(neg_lam, 0.0) + jnp.log(1.0 + jnp.exp(-jnp.abs(neg_lam)))
    row = lax.broadcasted_iota(jnp.int32, (ts, bw), 0)
    first_row = jnp.where(t == 0, 0, -1)
    for g in range(LRU_BLOCKS):
        cols = slice(g * bw, (g + 1) * bw)
        z = jnp.dot(xcb[:, cols], wax_ref[g], preferred_element_type=F32)
        r = jax.nn.sigmoid(z[:, 0:bw] + ba_ref[:, cols])
        gate_i = jax.nn.sigmoid(z[:, bw:2 * bw] + bx_ref[:, cols])
        log_a = (-LRU_C) * r * softplus[:, cols]
        th = jnp.tanh(log_a)
        mult = jnp.sqrt(-2.0 * th / (1.0 - th))
        mult = jnp.where(row == first_row, 1.0, mult)
        a_ref[:, cols] = jnp.exp(log_a)
        u_ref[:, cols] = mult * (gate_i * xc[:, cols])

    sub = lax.broadcasted_iota(jnp.int32, (SUBLANES, width), 0)

    def group(gidx, carry):
        off = pl.multiple_of(gidx * SUBLANES, SUBLANES)
        a = a_ref[pl.ds(off, SUBLANES), :]
        b = u_ref[pl.ds(off, SUBLANES), :]
        for k in (1, 2, 4):
            keep = sub >= k
            a_prev = jnp.where(keep, pltpu.roll(a, k, 0), 1.0)
            b_prev = jnp.where(keep, pltpu.roll(b, k, 0), 0.0)
            b = a * b_prev + b
            a = a * a_prev
        h = a * carry + b
        u_ref[pl.ds(off, SUBLANES), :] = h
        return jnp.broadcast_to(h[SUBLANES - 1:SUBLANES, :], h.shape)

    carry_ref[...] = lax.fori_loop(0, ts // SUBLANES, group, carry_ref[...])
    half = width // 2
    for gate_ref, cols in ((gate_lo_ref, slice(0, half)), (gate_hi_ref, slice(half, width))):
        o_ref[:, cols] = (u_ref[:, cols] * _silu(gate_ref[...].astype(F32))).astype(o_ref.dtype)


def _rg_lru(proj, conv_w, conv_b, w_ax, b_a, b_x, lam, *, batch, seq, col_x, col_gate, ts):
    n = batch * seq
    width = conv_b.shape[-1]
    nt = seq // ts
    half = width // 2
    bx, bg = _col_block(col_x, width), _col_block(col_gate, half)
    vec = lambda a: a.reshape(1, width)
    vmem = 2 * (3 * ts * width * 2 + w_ax.size * 2) + (3 * ts + CONV_HALO) * width * 4 + 8 * ts * width * 4 + (4 << 20)
    kernel = functools.partial(_lru_kernel, ts=ts)
    row_spec = lambda w, c: pl.BlockSpec((ts, w), lambda b, t: (b * nt + t, c))
    vec_spec = pl.BlockSpec((1, width), lambda b, t: (0, 0))
    return pl.pallas_call(
        kernel,
        grid=(batch, nt),
        in_specs=[
            row_spec(width, bx), row_spec(half, bg), row_spec(half, bg + 1),
            pl.BlockSpec(conv_w.shape, lambda b, t: (0, 0)),
            vec_spec,
            pl.BlockSpec(w_ax.shape, lambda b, t: (0, 0, 0)),
            vec_spec, vec_spec, vec_spec,
        ],
        out_specs=pl.BlockSpec((ts, width), lambda b, t: (b * nt + t, 0)),
        out_shape=jax.ShapeDtypeStruct((n, width), BF16),
        scratch_shapes=[
            pltpu.VMEM((CONV_HALO + ts, width), F32),
            pltpu.VMEM((ts, width), F32),
            pltpu.VMEM((ts, width), F32),
            pltpu.VMEM((SUBLANES, width), F32),
        ],
        compiler_params=_compiler_params(("arbitrary", "arbitrary"), vmem),
        name="rg_lru",
    )(proj, proj, proj, conv_w, vec(conv_b), w_ax, vec(b_a), vec(b_x), vec(lam))


def _mem_attn_kernel(q_ref, kv_ref, gate_ref, o_ref):
    scale = MEM_HEAD_DIM ** -0.5
    kv_width = MEM_HEADS * MEM_HEAD_DIM
    for hd in range(MEM_HEADS):
        cols = slice(hd * MEM_HEAD_DIM, (hd + 1) * MEM_HEAD_DIM)
        vcols = slice(kv_width + hd * MEM_HEAD_DIM, kv_width + (hd + 1) * MEM_HEAD_DIM)
        s = lax.dot_general(q_ref[:, cols], kv_ref[:, cols], (((1,), (1,)), ((), ())),
                            preferred_element_type=F32) * scale
        p = jnp.exp(s - jnp.max(s, axis=-1, keepdims=True))
        l = jnp.sum(p, axis=-1, keepdims=True)
        o = jnp.dot(p.astype(BF16), kv_ref[:, vcols], preferred_element_type=F32) / l
        o_ref[:, cols] = (o * _silu(gate_ref[:, cols].astype(F32))).astype(o_ref.dtype)


def _memory_attention(proj, mkv, *, seq, mem_len, col_q, col_gate, tq):
    n = proj.shape[0]
    width = MEM_HEADS * MEM_HEAD_DIM
    tiles_per_seq = seq // tq
    bq, bg = _col_block(col_q, width), _col_block(col_gate, width)
    vmem = 2 * (3 * tq * width * 2 + mem_len * 2 * width * 2) + 8 * tq * mem_len * 4 + (4 << 20)
    return pl.pallas_call(
        _mem_attn_kernel,
        grid=(n // tq,),
        in_specs=[
            pl.BlockSpec((tq, width), lambda i: (i, bq)),
            pl.BlockSpec((mem_len, 2 * width), lambda i: (i // tiles_per_seq, 0)),
            pl.BlockSpec((tq, width), lambda i: (i, bg)),
        ],
        out_specs=pl.BlockSpec((tq, width), lambda i: (i, 0)),
        out_shape=jax.ShapeDtypeStruct((n, width), BF16),
        compiler_params=_compiler_params(("parallel",), vmem),
        name="memory_attention",
    )(proj, mkv, proj)


def _merge_kernel(*refs):
    nb = (len(refs) - 1) // 3
    y_refs, w_refs, g_refs, o_ref = refs[:nb], refs[nb:2 * nb], refs[2 * nb:3 * nb], refs[-1]
    acc = None
    for y_ref, w_ref, g_ref in zip(y_refs, w_refs, g_refs):
        term = jax.nn.sigmoid(g_ref[...].astype(F32)) * jnp.dot(y_ref[...], w_ref[...], preferred_element_type=F32)
        acc = term if acc is None else acc + term
    o_ref[...] = acc.astype(o_ref.dtype)


def _merge(branches, proj, w_branch, *, col_merge, tm, tn):
    n = proj.shape[0]
    d = w_branch.shape[1]
    sizes = [y.shape[1] for y in branches]
    offsets = [sum(sizes[:i]) for i in range(len(sizes))]
    y_specs = [pl.BlockSpec((tm, sz), lambda i, j: (i, 0)) for sz in sizes]
    w_specs = [pl.BlockSpec((sz, tn), lambda i, j, r=_col_block(off, sz): (r, j)) for sz, off in zip(sizes, offsets)]
    g_specs = [pl.BlockSpec((tm, tn), lambda i, j, c=_col_block(col_merge + bi * d, tn): (i, c + j))
               for bi in range(len(sizes))]
    total = sum(sizes)
    vmem = 2 * (tm * total * 2 + total * tn * 2 + len(sizes) * tm * tn * 2 + tm * tn * 2) + 4 * tm * tn * 4 + (4 << 20)
    return pl.pallas_call(
        _merge_kernel,
        grid=(n // tm, d // tn),
        in_specs=y_specs + w_specs + g_specs,
        out_specs=pl.BlockSpec((tm, tn), lambda i, j: (i, j)),
        out_shape=jax.ShapeDtypeStruct((n, d), BF16),
        compiler_params=_compiler_params(("parallel", "arbitrary"), vmem),
        name="branch_merge",
    )(*branches, *([w_branch] * len(sizes)), *([proj] * len(sizes)))


def _out_proj_kernel(x_ref, m_ref, w_ref, g_ref, o_ref, *, final_norm):
    y = x_ref[...] + jnp.dot(m_ref[...], w_ref[...], preferred_element_type=F32)
    if final_norm:
        y = _rms(y, g_ref[...])
    o_ref[...] = y


def _out_proj(x, merged, w_out, final_g, *, final_norm, tm):
    n, d = x.shape
    vmem = 2 * (2 * tm * d * 4 + tm * d * 2 + d * d * 2) + 2 * tm * d * 4 + (4 << 20)
    kernel = functools.partial(_out_proj_kernel, final_norm=final_norm)
    return pl.pallas_call(
        kernel,
        grid=(n // tm,),
        in_specs=[
            pl.BlockSpec((tm, d), lambda i: (i, 0)),
            pl.BlockSpec((tm, d), lambda i: (i, 0)),
            pl.BlockSpec((d, d), lambda i: (0, 0)),
            pl.BlockSpec((1, d), lambda i: (0, 0)),
        ],
        out_specs=pl.BlockSpec((tm, d), lambda i: (i, 0)),
        out_shape=jax.ShapeDtypeStruct((n, d), F32),
        compiler_params=_compiler_params(("parallel",), vmem),
        name="out_proj",
    )(x, merged, w_out, final_g.reshape(1, d))


def kernel(x, mem, norm_g, w_in, lam_vecs, subln_g, pool_w, pool_scale, conv_w, conv_b, lru_wa, lru_ba,
           lru_wx, lru_bx, lru_lambda, mem_norm_g, w_mem_kv, w_branch, w_out, final_g):
    batch, seq, d = x.shape
    mem_len = mem.shape[1]
    depth = w_in.shape[0]
    da_width = DA_HEADS * DA_V_DIM
    pool_width = pool_scale.shape[-1]
    lru_width = conv_b.shape[-1]
    mem_width = MEM_HEADS * MEM_HEAD_DIM
    branch_width = da_width + pool_width + lru_width + mem_width
    col_q = 0
    col_k = col_q + da_width
    col_v = col_k + da_width
    col_pool = col_v + da_width
    col_lru = col_pool + pool_width
    col_qmem = col_lru + lru_width
    col_gate = col_qmem + mem_width
    col_merge = col_gate + branch_width
    gate_a = col_gate
    gate_b = gate_a + da_width
    gate_c = gate_b + pool_width
    gate_m = gate_c + lru_width

    xf = x.reshape(batch * seq, d)
    memf = mem.reshape(batch * mem_len, d)
    for l in range(depth):
        lambda_init = 0.8 - 0.6 * math.exp(-0.3 * l)
        proj = _norm_proj(xf, norm_g[l], w_in[l].astype(BF16), tm=1024, tn=1024, name="in_proj")
        mkv = _norm_proj(memf, mem_norm_g[l], w_mem_kv[l].astype(BF16), tm=batch * mem_len, tn=512, name="mem_kv")
        y_a = _diff_attention(proj, lam_vecs[l], subln_g[l], batch=batch, seq=seq, col_q=col_q, col_k=col_k,
                              col_v=col_v, col_gate=gate_a, lambda_init=lambda_init, tq=512)
        y_b = _multiscale_pool(proj, pool_w[l].astype(BF16), pool_scale[l], seq=seq, col_x=col_pool,
                               col_gate=gate_b, ts=512)
        w_ax = jnp.concatenate([lru_wa[l], lru_wx[l]], axis=-1).astype(BF16)
        y_c = _rg_lru(proj, conv_w[l], conv_b[l], w_ax, lru_ba[l], lru_bx[l], lru_lambda[l], batch=batch,
                      seq=seq, col_x=col_lru, col_gate=gate_c, ts=512)
        y_m = _memory_attention(proj, mkv, seq=seq, mem_len=mem_len, col_q=col_qmem, col_gate=gate_m, tq=512)
        merged = _merge([y_a, y_b, y_c, y_m], proj, w_branch[l].astype(BF16), col_merge=col_merge, tm=1024, tn=512)
        xf = _out_proj(xf, merged, w_out[l].astype(BF16), final_g, final_norm=(l == depth - 1), tm=512)
    return xf.reshape(batch, seq, d)
```

```python
import functools
import math

import jax
import jax.numpy as jnp
from jax import lax
from jax.experimental import pallas as pl
from jax.experimental.pallas import tpu as pltpu

F32 = jnp.float32
BF16 = jnp.bfloat16

EPS = 1e-6
DA_HEADS = 8
DA_HEAD_DIM = 64
DA_V_DIM = 2 * DA_HEAD_DIM
POOL_WINDOWS = (2, 4, 8, 16)
POOL_HALO = 16
LRU_BLOCKS = 8
CONV_WIDTH = 4
CONV_HALO = 8
LRU_C = 8.0
MEM_HEADS = 4
MEM_HEAD_DIM = 128
LANES = 128
SUBLANES = 8
V7X_VMEM_BYTES = 64 * 1024 * 1024
MASK_VALUE = -1e30
LOG2E = 1.4426950408889634


def _compiler_params(semantics, vmem_bytes):
    assert vmem_bytes < V7X_VMEM_BYTES
    return pltpu.CompilerParams(dimension_semantics=semantics, vmem_limit_bytes=int(vmem_bytes))


def _col_block(col, width):
    assert col % width == 0, (col, width)
    return col // width


def _rms(x, g):
    ms = jnp.mean(x * x, axis=-1, keepdims=True)
    return x * lax.rsqrt(ms + EPS) * g


def _lane_tile(x, n):
    return jnp.concatenate([x] * n, axis=1)


def _sigmoid(x):
    return 0.5 * jnp.tanh(0.5 * x) + 0.5


def _silu(x):
    return x * _sigmoid(x)


def _norm_proj_kernel(x_ref, g_ref, w_ref, o_ref, h_ref):
    @pl.when(pl.program_id(1) == 0)
    def _():
        h_ref[...] = _rms(x_ref[...], g_ref[...]).astype(BF16)

    o_ref[...] = jnp.dot(h_ref[...], w_ref[...], preferred_element_type=F32).astype(o_ref.dtype)


def _norm_proj(x, g, w, *, tm, tn, name):
    n, d = x.shape
    nc = w.shape[1]
    vmem = 2 * (tm * d * 4 + d * tn * 2 + tm * tn * 2) + tm * d * 2 + (4 << 20)
    return pl.pallas_call(
        _norm_proj_kernel,
        grid=(n // tm, nc // tn),
        in_specs=[
            pl.BlockSpec((tm, d), lambda i, j: (i, 0)),
            pl.BlockSpec((1, d), lambda i, j: (0, 0)),
            pl.BlockSpec((d, tn), lambda i, j: (0, j)),
        ],
        out_specs=pl.BlockSpec((tm, tn), lambda i, j: (i, j)),
        out_shape=jax.ShapeDtypeStruct((n, nc), BF16),
        scratch_shapes=[pltpu.VMEM((tm, d), BF16)],
        compiler_params=_compiler_params(("arbitrary", "arbitrary"), vmem),
        name=name,
    )(x, g.reshape(1, d), w)


def _norm_kernel(x_ref, g_ref, o_ref):
    o_ref[...] = _rms(x_ref[...], g_ref[...]).astype(o_ref.dtype)


def _norm(x, g, *, tm):
    n, d = x.shape
    vmem = 2 * (tm * d * 4 + tm * d * 2) + 3 * tm * d * 4 + (4 << 20)
    return pl.pallas_call(
        _norm_kernel,
        grid=(n // tm,),
        in_specs=[pl.BlockSpec((tm, d), lambda i: (i, 0)), pl.BlockSpec((1, d), lambda i: (0, 0))],
        out_specs=pl.BlockSpec((tm, d), lambda i: (i, 0)),
        out_shape=jax.ShapeDtypeStruct((n, d), BF16),
        compiler_params=_compiler_params(("arbitrary",), vmem),
        name="rms_norm",
    )(x, g.reshape(1, d))


def _proj_kernel(h_ref, w_ref, o_ref, wb_ref):
    @pl.when(pl.program_id(1) == 0)
    def _():
        wb_ref[...] = w_ref[...].astype(BF16)

    o_ref[...] = jnp.dot(h_ref[...], wb_ref[...], preferred_element_type=F32).astype(o_ref.dtype)


def _proj(h, w_stack, layer, *, tm, tn, name):
    n, d = h.shape
    nc = w_stack.shape[2]
    vmem = 2 * (tm * d * 2 + d * tn * 4 + tm * tn * 2) + d * tn * 2 + tm * tn * 4 + (4 << 20)
    return pl.pallas_call(
        _proj_kernel,
        grid=(nc // tn, n // tm),
        in_specs=[
            pl.BlockSpec((tm, d), lambda j, i: (i, 0)),
            pl.BlockSpec((None, d, tn), lambda j, i: (layer, 0, j)),
        ],
        out_specs=pl.BlockSpec((tm, tn), lambda j, i: (i, j)),
        out_shape=jax.ShapeDtypeStruct((n, nc), BF16),
        scratch_shapes=[pltpu.VMEM((d, tn), BF16)],
        compiler_params=_compiler_params(("arbitrary", "arbitrary"), vmem),
        name=name,
    )(h, w_stack)


def _diff_attn_kernel(q_ref, k_ref, v_ref, gate_ref, lamv_ref, subg_ref, o_ref,
                      qs_ref, vx_ref, m_ref, acc_ref, sa_ref, sb_ref, *, tq, lambda_init):
    qi = pl.program_id(2)
    rows = 2 * tq
    hw = DA_V_DIM
    scale2 = (DA_HEAD_DIM ** -0.5) * LOG2E

    @pl.when(qi == 0)
    def _():
        vx_ref[:, 0:hw] = v_ref[...]
        vx_ref[:, hw:2 * hw] = jnp.ones(v_ref.shape, v_ref.dtype)

    q = q_ref[...]
    lane = lax.broadcasted_iota(jnp.int32, q.shape, 1)
    zero = jnp.zeros_like(q)
    qs_ref[0:tq, :] = jnp.where(lane < DA_HEAD_DIM, q, zero)
    qs_ref[tq:rows, :] = jnp.where(lane >= DA_HEAD_DIM, q, zero)
    m_ref[...] = jnp.full(m_ref.shape, MASK_VALUE, F32)
    acc_ref[...] = jnp.zeros(acc_ref.shape, F32)

    def scores(j, s_ref):
        k = k_ref[pl.ds(pl.multiple_of(j * tq, tq), tq), :]
        s_ref[...] = lax.dot_general(qs_ref[...], k, (((1,), (1,)), ((), ())),
                                     preferred_element_type=F32)

    def update(j, s_ref, masked):
        s = s_ref[...]
        if masked:
            row = lax.broadcasted_iota(jnp.int32, s.shape, 0)
            col = lax.broadcasted_iota(jnp.int32, s.shape, 1)
            q_pos = jnp.where(row >= tq, row - tq, row)
            s = jnp.where(col <= q_pos, s, MASK_VALUE)
        m_prev = m_ref[...]
        m_next = jnp.maximum(m_prev, jnp.max(s, axis=-1, keepdims=True))
        p = jnp.exp2((s - _lane_tile(m_next, tq // LANES)) * scale2)
        alpha = jnp.exp2((m_prev - m_next) * scale2)
        vx = vx_ref[pl.ds(pl.multiple_of(j * tq, tq), tq), :]
        pv = jnp.dot(p.astype(BF16), vx, preferred_element_type=F32)
        acc_ref[...] = _lane_tile(alpha, 2) * acc_ref[...] + pv
        m_ref[...] = m_next

    scores(0, sa_ref)

    def unmasked_pair(jj, carry):
        scores(2 * jj + 1, sb_ref)
        update(2 * jj, sa_ref, False)
        scores(2 * jj + 2, sa_ref)
        update(2 * jj + 1, sb_ref, False)
        return carry

    lax.fori_loop(0, qi // 2, unmasked_pair, 0)

    @pl.when(qi % 2 == 0)
    def _():
        update(qi, sa_ref, True)

    @pl.when(qi % 2 == 1)
    def _():
        scores(qi, sb_ref)
        update(qi - 1, sa_ref, False)
        update(qi, sb_ref, True)

    o_maps = acc_ref[:, 0:hw] / acc_ref[:, hw:2 * hw]
    lv = lamv_ref[...]
    lam = (jnp.exp(jnp.sum(lv[0:1, :] * lv[1:2, :], axis=-1, keepdims=True))
           - jnp.exp(jnp.sum(lv[2:3, :] * lv[3:4, :], axis=-1, keepdims=True)) + lambda_init)
    o = o_maps[0:tq, :] - lam * o_maps[tq:rows, :]
    y = _rms(o, subg_ref[...]) * (1.0 - lambda_init)
    o_ref[...] = (y * _silu(gate_ref[...].astype(F32))).astype(o_ref.dtype)


def _diff_attention(proj, lam_vecs, subln_g, *, batch, seq, col_q, col_k, col_v, col_gate, lambda_init, tq):
    n = batch * seq
    nq = seq // tq
    hw = DA_V_DIM
    bq, bk, bv, bg = (_col_block(c, hw) for c in (col_q, col_k, col_v, col_gate))
    vmem = (2 * (2 * tq * hw * 2 + 2 * seq * hw * 2 + tq * hw * 2) + 2 * tq * hw * (2 + 12) + seq * 2 * hw * 2
            + 10 * 2 * tq * tq * 4 + (4 << 20))
    kernel = functools.partial(_diff_attn_kernel, tq=tq, lambda_init=lambda_init)
    return pl.pallas_call(
        kernel,
        grid=(batch, DA_HEADS, nq),
        in_specs=[
            pl.BlockSpec((tq, hw), lambda b, h, i: (b * nq + i, bq + h)),
            pl.BlockSpec((seq, hw), lambda b, h, i: (b, bk + h)),
            pl.BlockSpec((seq, hw), lambda b, h, i: (b, bv + h)),
            pl.BlockSpec((tq, hw), lambda b, h, i: (b * nq + i, bg + h)),
            pl.BlockSpec(lam_vecs.shape, lambda b, h, i: (0, 0)),
            pl.BlockSpec((1, hw), lambda b, h, i: (0, 0)),
        ],
        out_specs=pl.BlockSpec((tq, hw), lambda b, h, i: (b * nq + i, h)),
        out_shape=jax.ShapeDtypeStruct((n, DA_HEADS * hw), BF16),
        scratch_shapes=[
            pltpu.VMEM((2 * tq, hw), BF16),
            pltpu.VMEM((seq, 2 * hw), BF16),
            pltpu.VMEM((2 * tq, LANES), F32),
            pltpu.VMEM((2 * tq, 2 * hw), F32),
            pltpu.VMEM((2 * tq, tq), F32),
            pltpu.VMEM((2 * tq, tq), F32),
        ],
        compiler_params=_compiler_params(("arbitrary", "arbitrary", "arbitrary"), vmem),
        name="diff_attention",
    )(proj, proj, proj, proj, lam_vecs, subln_g.reshape(1, hw))


def _gate_cols(gate_refs, cols):
    piece = gate_refs[0].shape[1]
    idx, start = divmod(cols.start, piece)
    assert cols.stop - cols.start <= piece - start
    return _silu(gate_refs[idx][:, start:start + cols.stop - cols.start].astype(F32))


def _pool_kernel(x_ref, halo_ref, gate_lo_ref, gate_hi_ref, w_ref, scale_ref, o_ref, xe_ref, *, ts, tiles_per_seq):
    gate_refs = (gate_lo_ref, gate_hi_ref)
    t = pl.program_id(0) % tiles_per_seq
    halo = halo_ref[...].astype(F32)
    xe_ref[0:POOL_HALO, :] = jnp.where(t == 0, jnp.zeros_like(halo), halo)
    x = x_ref[...].astype(F32)
    xe_ref[POOL_HALO:POOL_HALO + ts, :] = x
    gw = x.shape[1] // len(POOL_WINDOWS)
    pos1 = t * ts + lax.broadcasted_iota(jnp.int32, (ts, gw), 0) + 1
    for gi, w in enumerate(POOL_WINDOWS):
        cols = slice(gi * gw, (gi + 1) * gw)
        xg = x[:, cols]
        wsum = xg
        for back in range(1, w):
            wsum = wsum + xe_ref[POOL_HALO - back:POOL_HALO - back + ts, cols]
        count = jnp.minimum(pos1, w).astype(F32)
        d = (wsum / count - xg).astype(BF16)
        y = jnp.dot(d, w_ref[gi], preferred_element_type=F32) * scale_ref[:, cols]
        o_ref[:, cols] = (y * _gate_cols(gate_refs, cols)).astype(o_ref.dtype)


def _multiscale_pool(proj, pool_w, pool_scale, *, seq, col_x, col_gate, ts):
    n = proj.shape[0]
    width = pool_scale.shape[-1]
    tiles_per_seq = seq // ts
    halo_per_tile = ts // POOL_HALO
    half = width // 2
    bx, bg = _col_block(col_x, width), _col_block(col_gate, half)
    vmem = 2 * (3 * ts * width * 2 + POOL_HALO * width * 2 + pool_w.size * 2) + 8 * ts * width * 4 + (4 << 20)
    kernel = functools.partial(_pool_kernel, ts=ts, tiles_per_seq=tiles_per_seq)
    return pl.pallas_call(
        kernel,
        grid=(n // ts,),
        in_specs=[
            pl.BlockSpec((ts, width), lambda i: (i, bx)),
            pl.BlockSpec((POOL_HALO, width), lambda i: (jnp.maximum(i * halo_per_tile - 1, 0), bx)),
            pl.BlockSpec((ts, half), lambda i: (i, bg)),
            pl.BlockSpec((ts, half), lambda i: (i, bg + 1)),
            pl.BlockSpec(pool_w.shape, lambda i: (0, 0, 0)),
            pl.BlockSpec((1, width), lambda i: (0, 0)),
        ],
        out_specs=pl.BlockSpec((ts, width), lambda i: (i, 0)),
        out_shape=jax.ShapeDtypeStruct((n, width), BF16),
        scratch_shapes=[pltpu.VMEM((POOL_HALO + ts, width), F32)],
        compiler_params=_compiler_params(("arbitrary",), vmem),
        name="multiscale_pool",
    )(proj, proj, proj, proj, pool_w, pool_scale.reshape(1, width))


def _lru_kernel(x_ref, gate_lo_ref, gate_hi_ref, cw_ref, cb_ref, wax_ref, ba_ref, bx_ref, lam_ref, o_ref,
                xe_ref, a_ref, u_ref, carry_ref, *, ts):
    t = pl.program_id(1)
    width = x_ref.shape[1]
    bw = width // LRU_BLOCKS

    @pl.when(t == 0)
    def _():
        xe_ref[0:CONV_HALO, :] = jnp.zeros((CONV_HALO, width), F32)
        carry_ref[...] = jnp.zeros(carry_ref.shape, F32)

    @pl.when(t > 0)
    def _():
        xe_ref[0:CONV_HALO, :] = xe_ref[ts:ts + CONV_HALO, :]

    xe_ref[CONV_HALO:CONV_HALO + ts, :] = x_ref[...].astype(F32)

    xc = cb_ref[...] + cw_ref[CONV_WIDTH - 1:CONV_WIDTH, :] * xe_ref[CONV_HALO:CONV_HALO + ts, :]
    for back in range(1, CONV_WIDTH):
        tap = CONV_WIDTH - 1 - back
        xc = xc + cw_ref[tap:tap + 1, :] * xe_ref[CONV_HALO - back:CONV_HALO - back + ts, :]
    xcb = xc.astype(BF16)

    neg_lam = -lam_ref[...]
    softplus = jnp.maximum(neg_lam, 0.0) + jnp.log(1.0 + jnp.exp(-jnp.abs(neg_lam)))
    row = lax.broadcasted_iota(jnp.int32, (ts, bw), 0)
    first_row = jnp.where(t == 0, 0, -1)
    for g in range(LRU_BLOCKS):
        cols = slice(g * bw, (g + 1) * bw)
        z = jnp.dot(xcb[:, cols], wax_ref[g], preferred_element_type=F32)
        r = _sigmoid(z[:, 0:bw] + ba_ref[:, cols])
        gate_i = _sigmoid(z[:, bw:2 * bw] + bx_ref[:, cols])
        log_a = (-LRU_C) * r * softplus[:, cols]
        th = jnp.tanh(log_a)
        mult = jnp.sqrt(-2.0 * th / (1.0 - th))
        mult = jnp.where(row == first_row, 1.0, mult)
        a_ref[:, cols] = jnp.exp(log_a)
        u_ref[:, cols] = mult * (gate_i * xc[:, cols])

    sub = lax.broadcasted_iota(jnp.int32, (SUBLANES, width), 0)

    def group(gidx, carry):
        off = pl.multiple_of(gidx * SUBLANES, SUBLANES)
        a = a_ref[pl.ds(off, SUBLANES), :]
        b = u_ref[pl.ds(off, SUBLANES), :]
        for k in (1, 2, 4):
            keep = sub >= k
            a_prev = jnp.where(keep, pltpu.roll(a, k, 0), 1.0)
            b_prev = jnp.where(keep, pltpu.roll(b, k, 0), 0.0)
            b = a * b_prev + b
            a = a * a_prev
        h = a * carry + b
        u_ref[pl.ds(off, SUBLANES), :] = h
        return jnp.broadcast_to(h[SUBLANES - 1:SUBLANES, :], h.shape)

    carry_ref[...] = lax.fori_loop(0, ts // SUBLANES, group, carry_ref[...])
    half = width // 2
    for gate_ref, cols in ((gate_lo_ref, slice(0, half)), (gate_hi_ref, slice(half, width))):
        o_ref[:, cols] = (u_ref[:, cols] * _silu(gate_ref[...].astype(F32))).astype(o_ref.dtype)


def _rg_lru(proj, conv_w, conv_b, w_ax, b_a, b_x, lam, *, batch, seq, col_x, col_gate, ts):
    n = batch * seq
    width = conv_b.shape[-1]
    nt = seq // ts
    half = width // 2
    bx, bg = _col_block(col_x, width), _col_block(col_gate, half)
    vec = lambda a: a.reshape(1, width)
    vmem = 2 * (3 * ts * width * 2 + w_ax.size * 2) + (3 * ts + CONV_HALO) * width * 4 + 8 * ts * width * 4 + (4 << 20)
    kernel = functools.partial(_lru_kernel, ts=ts)
    row_spec = lambda w, c: pl.BlockSpec((ts, w), lambda b, t: (b * nt + t, c))
    vec_spec = pl.BlockSpec((1, width), lambda b, t: (0, 0))
    return pl.pallas_call(
        kernel,
        grid=(batch, nt),
        in_specs=[
            row_spec(width, bx), row_spec(half, bg), row_spec(half, bg + 1),
            pl.BlockSpec(conv_w.shape, lambda b, t: (0, 0)),
            vec_spec,
            pl.BlockSpec(w_ax.shape, lambda b, t: (0, 0, 0)),
            vec_spec, vec_spec, vec_spec,
        ],
        out_specs=pl.BlockSpec((ts, width), lambda b, t: (b * nt + t, 0)),
        out_shape=jax.ShapeDtypeStruct((n, width), BF16),
        scratch_shapes=[
            pltpu.VMEM((CONV_HALO + ts, width), F32),
            pltpu.VMEM((ts, width), F32),
            pltpu.VMEM((ts, width), F32),
            pltpu.VMEM((SUBLANES, width), F32),
        ],
        compiler_params=_compiler_params(("arbitrary", "arbitrary"), vmem),
        name="rg_lru",
    )(proj, proj, proj, conv_w, vec(conv_b), w_ax, vec(b_a), vec(b_x), vec(lam))


def _mem_attn_kernel(q_ref, kv_ref, gate_ref, o_ref):
    scale = MEM_HEAD_DIM ** -0.5
    kv_width = MEM_HEADS * MEM_HEAD_DIM
    for hd in range(MEM_HEADS):
        cols = slice(hd * MEM_HEAD_DIM, (hd + 1) * MEM_HEAD_DIM)
        vcols = slice(kv_width + hd * MEM_HEAD_DIM, kv_width + (hd + 1) * MEM_HEAD_DIM)
        s = lax.dot_general(q_ref[:, cols], kv_ref[:, cols], (((1,), (1,)), ((), ())),
                            preferred_element_type=F32) * scale
        p = jnp.exp(s - jnp.max(s, axis=-1, keepdims=True))
        l = jnp.sum(p, axis=-1, keepdims=True)
        o = jnp.dot(p.astype(BF16), kv_ref[:, vcols], preferred_element_type=F32) / l
        o_ref[:, cols] = (o * _silu(gate_ref[:, cols].astype(F32))).astype(o_ref.dtype)


def _memory_attention(proj, mkv, *, seq, mem_len, col_q, col_gate, tq):
    n = proj.shape[0]
    width = MEM_HEADS * MEM_HEAD_DIM
    tiles_per_seq = seq // tq
    bq, bg = _col_block(col_q, width), _col_block(col_gate, width)
    vmem = 2 * (3 * tq * width * 2 + mem_len * 2 * width * 2) + 8 * tq * mem_len * 4 + (4 << 20)
    return pl.pallas_call(
        _mem_attn_kernel,
        grid=(n // tq,),
        in_specs=[
            pl.BlockSpec((tq, width), lambda i: (i, bq)),
            pl.BlockSpec((mem_len, 2 * width), lambda i: (i // tiles_per_seq, 0)),
            pl.BlockSpec((tq, width), lambda i: (i, bg)),
        ],
        out_specs=pl.BlockSpec((tq, width), lambda i: (i, 0)),
        out_shape=jax.ShapeDtypeStruct((n, width), BF16),
        compiler_params=_compiler_params(("arbitrary",), vmem),
        name="memory_attention",
    )(proj, mkv, proj)


def _merge_kernel(*refs):
    nb = (len(refs) - 1) // 3
    y_refs, w_refs, g_refs, o_ref = refs[:nb], refs[nb:2 * nb], refs[2 * nb:3 * nb], refs[-1]
    acc = None
    for y_ref, w_ref, g_ref in zip(y_refs, w_refs, g_refs):
        term = _sigmoid(g_ref[...].astype(F32)) * jnp.dot(y_ref[...], w_ref[...], preferred_element_type=F32)
        acc = term if acc is None else acc + term
    o_ref[...] = acc.astype(o_ref.dtype)


def _merge(branches, proj, w_branch, *, col_merge, tm, tn):
    n = proj.shape[0]
    d = w_branch.shape[1]
    sizes = [y.shape[1] for y in branches]
    offsets = [sum(sizes[:i]) for i in range(len(sizes))]
    y_specs = [pl.BlockSpec((tm, sz), lambda i, j: (i, 0)) for sz in sizes]
    w_specs = [pl.BlockSpec((sz, tn), lambda i, j, r=_col_block(off, sz): (r, j)) for sz, off in zip(sizes, offsets)]
    g_specs = [pl.BlockSpec((tm, tn), lambda i, j, c=_col_block(col_merge + bi * d, tn): (i, c + j))
               for bi in range(len(sizes))]
    total = sum(sizes)
    vmem = 2 * (tm * total * 2 + total * tn * 2 + len(sizes) * tm * tn * 2 + tm * tn * 2) + 4 * tm * tn * 4 + (4 << 20)
    return pl.pallas_call(
        _merge_kernel,
        grid=(n // tm, d // tn),
        in_specs=y_specs + w_specs + g_specs,
        out_specs=pl.BlockSpec((tm, tn), lambda i, j: (i, j)),
        out_shape=jax.ShapeDtypeStruct((n, d), BF16),
        compiler_params=_compiler_params(("arbitrary", "arbitrary"), vmem),
        name="branch_merge",
    )(*branches, *([w_branch] * len(sizes)), *([proj] * len(sizes)))


def _out_proj_kernel(x_ref, m_ref, w_ref, g_ref, *o_refs, last):
    y = x_ref[...] + jnp.dot(m_ref[...], w_ref[...], preferred_element_type=F32)
    normed = _rms(y, g_ref[...])
    if last:
        o_refs[0][...] = normed
    else:
        o_refs[0][...] = y
        o_refs[1][...] = normed.astype(BF16)


def _out_proj(x, merged, w_out, g, *, last, tm):
    n, d = x.shape
    vmem = 2 * (2 * tm * d * 4 + 2 * tm * d * 2 + d * d * 2) + 3 * tm * d * 4 + (4 << 20)
    row_spec = pl.BlockSpec((tm, d), lambda i: (i, 0))
    out_shape = [jax.ShapeDtypeStruct((n, d), F32)] + ([] if last else [jax.ShapeDtypeStruct((n, d), BF16)])
    return pl.pallas_call(
        functools.partial(_out_proj_kernel, last=last),
        grid=(n // tm,),
        in_specs=[row_spec, row_spec, pl.BlockSpec((d, d), lambda i: (0, 0)), pl.BlockSpec((1, d), lambda i: (0, 0))],
        out_specs=[row_spec] * len(out_shape),
        out_shape=out_shape,
        compiler_params=_compiler_params(("arbitrary",), vmem),
        name="out_proj",
    )(x, merged, w_out, g.reshape(1, d))


def kernel(x, mem, norm_g, w_in, lam_vecs, subln_g, pool_w, pool_scale, conv_w, conv_b, lru_wa, lru_ba,
           lru_wx, lru_bx, lru_lambda, mem_norm_g, w_mem_kv, w_branch, w_out, final_g):
    batch, seq, d = x.shape
    mem_len = mem.shape[1]
    depth = w_in.shape[0]
    da_width = DA_HEADS * DA_V_DIM
    pool_width = pool_scale.shape[-1]
    lru_width = conv_b.shape[-1]
    mem_width = MEM_HEADS * MEM_HEAD_DIM
    branch_width = da_width + pool_width + lru_width + mem_width
    col_q = 0
    col_k = col_q + da_width
    col_v = col_k + da_width
    col_pool = col_v + da_width
    col_lru = col_pool + pool_width
    col_qmem = col_lru + lru_width
    col_gate = col_qmem + mem_width
    col_merge = col_gate + branch_width
    gate_a = col_gate
    gate_b = gate_a + da_width
    gate_c = gate_b + pool_width
    gate_m = gate_c + lru_width

    xf = x.reshape(batch * seq, d)
    memf = mem.reshape(batch * mem_len, d)
    h = _norm(xf, norm_g[0], tm=512)
    for l in range(depth):
        last = l == depth - 1
        lambda_init = 0.8 - 0.6 * math.exp(-0.3 * l)
        proj = _proj(h, w_in, l, tm=1024, tn=1024, name="in_proj")
        mkv = _norm_proj(memf, mem_norm_g[l], w_mem_kv[l].astype(BF16), tm=batch * mem_len, tn=512, name="mem_kv")
        y_a = _diff_attention(proj, lam_vecs[l], subln_g[l], batch=batch, seq=seq, col_q=col_q, col_k=col_k,
                              col_v=col_v, col_gate=gate_a, lambda_init=lambda_init, tq=512)
        y_b = _multiscale_pool(proj, pool_w[l].astype(BF16), pool_scale[l], seq=seq, col_x=col_pool,
                               col_gate=gate_b, ts=512)
        w_ax = jnp.concatenate([lru_wa[l], lru_wx[l]], axis=-1).astype(BF16)
        y_c = _rg_lru(proj, conv_w[l], conv_b[l], w_ax, lru_ba[l], lru_bx[l], lru_lambda[l], batch=batch,
                      seq=seq, col_x=col_lru, col_gate=gate_c, ts=512)
        y_m = _memory_attention(proj, mkv, seq=seq, mem_len=mem_len, col_q=col_qmem, col_gate=gate_m, tq=512)
        merged = _merge([y_a, y_b, y_c, y_m], proj, w_branch[l].astype(BF16), col_merge=col_merge, tm=1024, tn=512)
        if last:
            (out,) = _out_proj(xf, merged, w_out[l].astype(BF16), final_g, last=True, tm=512)
        else:
            xf, h = _out_proj(xf, merged, w_out[l].astype(BF16), norm_g[l + 1], last=False, tm=512)
    return out.reshape(batch, seq, d)
```

```python
import functools
import math

import jax
import jax.numpy as jnp
from jax import lax
from jax.experimental import pallas as pl
from jax.experimental.pallas import tpu as pltpu

F32 = jnp.float32
BF16 = jnp.bfloat16

EPS = 1e-6
DA_HEADS = 8
DA_HEAD_DIM = 64
DA_V_DIM = 2 * DA_HEAD_DIM
POOL_WINDOWS = (2, 4, 8, 16)
POOL_HALO = 16
LRU_BLOCKS = 8
CONV_WIDTH = 4
CONV_HALO = 8
LRU_C = 8.0
MEM_HEADS = 4
MEM_HEAD_DIM = 128
LANES = 128
SUBLANES = 8
V7X_VMEM_BYTES = 64 * 1024 * 1024
PROJ_TN = 1024
MASK_VALUE = -1e30
LOG2E = 1.4426950408889634


def _compiler_params(semantics, vmem_bytes):
    assert vmem_bytes < V7X_VMEM_BYTES
    return pltpu.CompilerParams(dimension_semantics=semantics, vmem_limit_bytes=int(vmem_bytes))


def _col_block(col, width):
    assert col % width == 0, (col, width)
    return col // width


def _rms(x, g):
    ms = jnp.mean(x * x, axis=-1, keepdims=True)
    return x * lax.rsqrt(ms + EPS) * g


def _lane_tile(x, n):
    return jnp.concatenate([x] * n, axis=1)


def _sigmoid(x):
    return 0.5 * jnp.tanh(0.5 * x) + 0.5


def _silu(x):
    return x * _sigmoid(x)


def _norm_proj_kernel(x_ref, g_ref, w_ref, o_ref, h_ref):
    @pl.when(pl.program_id(1) == 0)
    def _():
        h_ref[...] = _rms(x_ref[...], g_ref[...]).astype(BF16)

    o_ref[...] = jnp.dot(h_ref[...], w_ref[...], preferred_element_type=F32).astype(o_ref.dtype)


def _norm_proj(x, g, w, *, tm, tn, name):
    n, d = x.shape
    nc = w.shape[1]
    vmem = 2 * (tm * d * 4 + d * tn * 2 + tm * tn * 2) + tm * d * 2 + (4 << 20)
    return pl.pallas_call(
        _norm_proj_kernel,
        grid=(n // tm, nc // tn),
        in_specs=[
            pl.BlockSpec((tm, d), lambda i, j: (i, 0)),
            pl.BlockSpec((1, d), lambda i, j: (0, 0)),
            pl.BlockSpec((d, tn), lambda i, j: (0, j)),
        ],
        out_specs=pl.BlockSpec((tm, tn), lambda i, j: (i, j)),
        out_shape=jax.ShapeDtypeStruct((n, nc), BF16),
        scratch_shapes=[pltpu.VMEM((tm, d), BF16)],
        compiler_params=_compiler_params(("arbitrary", "arbitrary"), vmem),
        name=name,
    )(x, g.reshape(1, d), w)


def _norm_kernel(x_ref, g_ref, o_ref):
    o_ref[...] = _rms(x_ref[...], g_ref[...]).astype(o_ref.dtype)


def _norm(x, g, *, tm):
    n, d = x.shape
    vmem = 2 * (tm * d * 4 + tm * d * 2) + 3 * tm * d * 4 + (4 << 20)
    return pl.pallas_call(
        _norm_kernel,
        grid=(n // tm,),
        in_specs=[pl.BlockSpec((tm, d), lambda i: (i, 0)), pl.BlockSpec((1, d), lambda i: (0, 0))],
        out_specs=pl.BlockSpec((tm, d), lambda i: (i, 0)),
        out_shape=jax.ShapeDtypeStruct((n, d), BF16),
        compiler_params=_compiler_params(("arbitrary",), vmem),
        name="rms_norm",
    )(x, g.reshape(1, d))


def _proj_kernel(h_ref, w_ref, o_ref, wb_ref):
    @pl.when(pl.program_id(1) == 0)
    def _():
        wb_ref[...] = w_ref[...].astype(BF16)

    o_ref[...] = jnp.dot(h_ref[...], wb_ref[...], preferred_element_type=F32).astype(o_ref.dtype)


def _proj(h, w_stack, layer, *, panels, tm, tn, name):
    n, d = h.shape
    vmem = 2 * (tm * d * 2 + d * tn * 4 + tm * tn * 2) + d * tn * 2 + tm * tn * 4 + (4 << 20)
    return pl.pallas_call(
        _proj_kernel,
        grid=(panels, n // tm),
        in_specs=[
            pl.BlockSpec((tm, d), lambda j, i: (i, 0)),
            pl.BlockSpec((None, d, tn), lambda j, i: (layer, 0, j)),
        ],
        out_specs=pl.BlockSpec((tm, tn), lambda j, i: (i, j)),
        out_shape=jax.ShapeDtypeStruct((n, panels * tn), BF16),
        scratch_shapes=[pltpu.VMEM((d, tn), BF16)],
        compiler_params=_compiler_params(("arbitrary", "arbitrary"), vmem),
        name=name,
    )(h, w_stack)


def _diff_attn_kernel(q_ref, k_ref, v_ref, gate_ref, lamv_ref, subg_ref, o_ref,
                      qs_ref, vx_ref, m_ref, acc_ref, sa_ref, sb_ref, *, tq, lambda_init):
    g = pl.program_id(2)
    rows = 2 * tq
    hw = DA_V_DIM
    scale2 = (DA_HEAD_DIM ** -0.5) * LOG2E

    @pl.when(g == 0)
    def _():
        vx_ref[:, 0:hw] = v_ref[...]
        vx_ref[:, hw:2 * hw] = jnp.ones(v_ref.shape, v_ref.dtype)

    for t in range(2):
        q = q_ref[t * tq:(t + 1) * tq, :]
        lane = lax.broadcasted_iota(jnp.int32, q.shape, 1)
        zero = jnp.zeros_like(q)
        qs_ref[t, 0:tq, :] = jnp.where(lane < DA_HEAD_DIM, q, zero)
        qs_ref[t, tq:rows, :] = jnp.where(lane >= DA_HEAD_DIM, q, zero)
    m_ref[...] = jnp.full(m_ref.shape, MASK_VALUE, F32)
    acc_ref[...] = jnp.zeros(acc_ref.shape, F32)

    def scores(t, j, s_ref):
        k = k_ref[pl.ds(pl.multiple_of(j * tq, tq), tq), :]
        s_ref[...] = lax.dot_general(qs_ref[t], k, (((1,), (1,)), ((), ())),
                                     preferred_element_type=F32)

    def update(t, j, s_ref, masked):
        s = s_ref[...]
        if masked:
            row = lax.broadcasted_iota(jnp.int32, s.shape, 0)
            col = lax.broadcasted_iota(jnp.int32, s.shape, 1)
            q_pos = jnp.where(row >= tq, row - tq, row)
            s = jnp.where(col <= q_pos, s, MASK_VALUE)
        m_prev = m_ref[t]
        m_next = jnp.maximum(m_prev, jnp.max(s, axis=-1, keepdims=True))
        p = jnp.exp2((s - _lane_tile(m_next, tq // LANES)) * scale2)
        alpha = jnp.exp2((m_prev - m_next) * scale2)
        vx = vx_ref[pl.ds(pl.multiple_of(j * tq, tq), tq), :]
        pv = jnp.dot(p.astype(BF16), vx, preferred_element_type=F32)
        acc_ref[t] = _lane_tile(alpha, 2) * acc_ref[t] + pv
        m_ref[t] = m_next

    lv = lamv_ref[...]
    lam = (jnp.exp(jnp.sum(lv[0:1, :] * lv[1:2, :], axis=-1, keepdims=True))
           - jnp.exp(jnp.sum(lv[2:3, :] * lv[3:4, :], axis=-1, keepdims=True)) + lambda_init)

    def finalize(t):
        o_maps = acc_ref[t, :, 0:hw] / acc_ref[t, :, hw:2 * hw]
        o = o_maps[0:tq, :] - lam * o_maps[tq:rows, :]
        y = _rms(o, subg_ref[...]) * (1.0 - lambda_init)
        out_rows = slice(t * tq, (t + 1) * tq)
        o_ref[out_rows, :] = (y * _silu(gate_ref[out_rows, :].astype(F32))).astype(o_ref.dtype)

    def unmasked_pairs(t, first_ref, second_ref):
        def body(jj, carry):
            scores(t, 2 * jj + 1, second_ref)
            update(t, 2 * jj, first_ref, False)
            scores(t, 2 * jj + 2, first_ref)
            update(t, 2 * jj + 1, second_ref, False)
            return carry

        lax.fori_loop(0, g, body, 0)

    scores(0, 0, sa_ref)
    unmasked_pairs(0, sa_ref, sb_ref)
    scores(1, 0, sb_ref)
    update(0, 2 * g, sa_ref, True)
    finalize(0)
    unmasked_pairs(1, sb_ref, sa_ref)
    scores(1, 2 * g + 1, sa_ref)
    update(1, 2 * g, sb_ref, False)
    update(1, 2 * g + 1, sa_ref, True)
    finalize(1)


def _diff_attention(proj, lam_vecs, subln_g, *, batch, seq, col_q, col_k, col_v, col_gate, lambda_init, tq):
    n = batch * seq
    assert seq % (2 * tq) == 0
    ng = seq // (2 * tq)
    hw = DA_V_DIM
    bq, bk, bv, bg = (_col_block(c, hw) for c in (col_q, col_k, col_v, col_gate))
    vmem = (2 * (3 * 2 * tq * hw * 2 + 2 * seq * hw * 2) + 2 * 2 * tq * hw * (2 + 12) + seq * 2 * hw * 2
            + 10 * 2 * tq * tq * 4 + (4 << 20))
    kernel = functools.partial(_diff_attn_kernel, tq=tq, lambda_init=lambda_init)
    return pl.pallas_call(
        kernel,
        grid=(batch, DA_HEADS, ng),
        in_specs=[
            pl.BlockSpec((2 * tq, hw), lambda b, h, i: (b * ng + i, bq + h)),
            pl.BlockSpec((seq, hw), lambda b, h, i: (b, bk + h)),
            pl.BlockSpec((seq, hw), lambda b, h, i: (b, bv + h)),
            pl.BlockSpec((2 * tq, hw), lambda b, h, i: (b * ng + i, bg + h)),
            pl.BlockSpec(lam_vecs.shape, lambda b, h, i: (0, 0)),
            pl.BlockSpec((1, hw), lambda b, h, i: (0, 0)),
        ],
        out_specs=pl.BlockSpec((2 * tq, hw), lambda b, h, i: (b * ng + i, h)),
        out_shape=jax.ShapeDtypeStruct((n, DA_HEADS * hw), BF16),
        scratch_shapes=[
            pltpu.VMEM((2, 2 * tq, hw), BF16),
            pltpu.VMEM((seq, 2 * hw), BF16),
            pltpu.VMEM((2, 2 * tq, LANES), F32),
            pltpu.VMEM((2, 2 * tq, 2 * hw), F32),
            pltpu.VMEM((2 * tq, tq), F32),
            pltpu.VMEM((2 * tq, tq), F32),
        ],
        compiler_params=_compiler_params(("arbitrary", "arbitrary", "arbitrary"), vmem),
        name="diff_attention",
    )(proj, proj, proj, proj, lam_vecs, subln_g.reshape(1, hw))


def _gate_cols(gate_refs, cols):
    piece = gate_refs[0].shape[1]
    idx, start = divmod(cols.start, piece)
    assert cols.stop - cols.start <= piece - start
    return _silu(gate_refs[idx][:, start:start + cols.stop - cols.start].astype(F32))


def _pool_kernel(x_ref, halo_ref, gate_lo_ref, gate_hi_ref, w_ref, scale_ref, o_ref, xe_ref, *, ts, tiles_per_seq):
    gate_refs = (gate_lo_ref, gate_hi_ref)
    t = pl.program_id(0) % tiles_per_seq
    halo = halo_ref[...].astype(F32)
    xe_ref[0:POOL_HALO, :] = jnp.where(t == 0, jnp.zeros_like(halo), halo)
    x = x_ref[...].astype(F32)
    xe_ref[POOL_HALO:POOL_HALO + ts, :] = x
    gw = x.shape[1] // len(POOL_WINDOWS)
    pos1 = t * ts + lax.broadcasted_iota(jnp.int32, (ts, gw), 0) + 1
    for gi, w in enumerate(POOL_WINDOWS):
        cols = slice(gi * gw, (gi + 1) * gw)
        xg = x[:, cols]
        wsum = xg
        for back in range(1, w):
            wsum = wsum + xe_ref[POOL_HALO - back:POOL_HALO - back + ts, cols]
        count = jnp.minimum(pos1, w).astype(F32)
        d = (wsum / count - xg).astype(BF16)
        y = jnp.dot(d, w_ref[gi], preferred_element_type=F32) * scale_ref[:, cols]
        o_ref[:, cols] = (y * _gate_cols(gate_refs, cols)).astype(o_ref.dtype)


def _multiscale_pool(proj, pool_w, pool_scale, *, seq, col_x, col_gate, ts):
    n = proj.shape[0]
    width = pool_scale.shape[-1]
    tiles_per_seq = seq // ts
    halo_per_tile = ts // POOL_HALO
    half = width // 2
    bx, bg = _col_block(col_x, width), _col_block(col_gate, half)
    vmem = 2 * (3 * ts * width * 2 + POOL_HALO * width * 2 + pool_w.size * 2) + 8 * ts * width * 4 + (4 << 20)
    kernel = functools.partial(_pool_kernel, ts=ts, tiles_per_seq=tiles_per_seq)
    return pl.pallas_call(
        kernel,
        grid=(n // ts,),
        in_specs=[
            pl.BlockSpec((ts, width), lambda i: (i, bx)),
            pl.BlockSpec((POOL_HALO, width), lambda i: (jnp.maximum(i * halo_per_tile - 1, 0), bx)),
            pl.BlockSpec((ts, half), lambda i: (i, bg)),
            pl.BlockSpec((ts, half), lambda i: (i, bg + 1)),
            pl.BlockSpec(pool_w.shape, lambda i: (0, 0, 0)),
            pl.BlockSpec((1, width), lambda i: (0, 0)),
        ],
        out_specs=pl.BlockSpec((ts, width), lambda i: (i, 0)),
        out_shape=jax.ShapeDtypeStruct((n, width), BF16),
        scratch_shapes=[pltpu.VMEM((POOL_HALO + ts, width), F32)],
        compiler_params=_compiler_params(("arbitrary",), vmem),
        name="multiscale_pool",
    )(proj, proj, proj, proj, pool_w, pool_scale.reshape(1, width))


def _proj_lru_kernel(h_ref, w_ref, x_ref, gate_ref, cw_ref, cb_ref, wax_ref, ba_ref, bx_ref, lam_ref,
                     o_ref, y_ref, wb_ref, xe_ref, carry_ref, *, tiles_per_seq):
    i = pl.program_id(1)
    t = i % tiles_per_seq
    ts, bw = x_ref.shape

    @pl.when(i == 0)
    def _():
        wb_ref[...] = w_ref[...].astype(BF16)

    @pl.when(t == 0)
    def _():
        xe_ref[0:CONV_HALO, :] = jnp.zeros((CONV_HALO, bw), F32)
        carry_ref[...] = jnp.zeros(carry_ref.shape, F32)

    @pl.when(t > 0)
    def _():
        xe_ref[0:CONV_HALO, :] = xe_ref[ts:ts + CONV_HALO, :]

    xe_ref[CONV_HALO:CONV_HALO + ts, :] = x_ref[...].astype(F32)

    xc = cb_ref[...] + cw_ref[CONV_WIDTH - 1:CONV_WIDTH, :] * xe_ref[CONV_HALO:CONV_HALO + ts, :]
    for back in range(1, CONV_WIDTH):
        tap = CONV_WIDTH - 1 - back
        xc = xc + cw_ref[tap:tap + 1, :] * xe_ref[CONV_HALO - back:CONV_HALO - back + ts, :]

    neg_lam = -lam_ref[...]
    softplus = jnp.maximum(neg_lam, 0.0) + jnp.log(1.0 + jnp.exp(-jnp.abs(neg_lam)))
    z = jnp.dot(xc.astype(BF16), wax_ref[...], preferred_element_type=F32)
    o_ref[...] = jnp.dot(h_ref[...], wb_ref[...], preferred_element_type=F32).astype(o_ref.dtype)

    r = _sigmoid(z[:, 0:bw] + ba_ref[...])
    gate_i = _sigmoid(z[:, bw:2 * bw] + bx_ref[...])
    log_a = (-LRU_C) * r * softplus
    th = jnp.tanh(log_a)
    m2 = -2.0 * th / (1.0 - th)
    mult = jnp.where(m2 > 0.0, m2 * lax.rsqrt(m2), 0.0)
    row = lax.broadcasted_iota(jnp.int32, (ts, bw), 0)
    first_row = jnp.where(t == 0, 0, -1)
    mult = jnp.where(row == first_row, 1.0, mult)
    a_all = jnp.exp(log_a)
    u_all = mult * (gate_i * xc)

    sub = lax.broadcasted_iota(jnp.int32, (SUBLANES, bw), 0)
    carry = carry_ref[...]
    hs = []
    for gidx in range(ts // SUBLANES):
        rows = slice(gidx * SUBLANES, (gidx + 1) * SUBLANES)
        a, b = a_all[rows, :], u_all[rows, :]
        for k in (1, 2, 4):
            keep = sub >= k
            a_prev = jnp.where(keep, pltpu.roll(a, k, 0), 1.0)
            b_prev = jnp.where(keep, pltpu.roll(b, k, 0), 0.0)
            b = a * b_prev + b
            a = a * a_prev
        h = a * carry + b
        hs.append(h)
        carry = jnp.broadcast_to(h[SUBLANES - 1:SUBLANES, :], h.shape)
    carry_ref[...] = carry
    y_ref[...] = (jnp.concatenate(hs, axis=0) * _silu(gate_ref[...].astype(F32))).astype(y_ref.dtype)


def _proj_lru(h, w_stack, layer, proj_a, conv_w, conv_b, w_ax, b_a, b_x, lam, *, seq, panel0, col_x, col_gate,
              tm, tn):
    n, d = h.shape
    assert w_stack.shape[2] == (panel0 + LRU_BLOCKS) * tn and seq % tm == 0
    width = conv_b.shape[-1]
    bw = width // LRU_BLOCKS
    bx, bg = _col_block(col_x, bw), _col_block(col_gate, bw)
    vec = lambda a: a.reshape(1, width)
    vmem = (2 * (tm * d * 2 + d * tn * 4 + tm * tn * 2) + d * tn * 2 + tm * tn * 4
            + 24 * tm * bw * 4 + (4 << 20))
    kernel = functools.partial(_proj_lru_kernel, tiles_per_seq=seq // tm)
    vec_spec = pl.BlockSpec((1, bw), lambda p, i: (0, p))
    return pl.pallas_call(
        kernel,
        grid=(LRU_BLOCKS, n // tm),
        in_specs=[
            pl.BlockSpec((tm, d), lambda p, i: (i, 0)),
            pl.BlockSpec((None, d, tn), lambda p, i: (layer, 0, panel0 + p)),
            pl.BlockSpec((tm, bw), lambda p, i: (i, bx + p)),
            pl.BlockSpec((tm, bw), lambda p, i: (i, bg + p)),
            pl.BlockSpec((CONV_WIDTH, bw), lambda p, i: (0, p)),
            vec_spec,
            pl.BlockSpec((None, bw, 2 * bw), lambda p, i: (p, 0, 0)),
            vec_spec, vec_spec, vec_spec,
        ],
        out_specs=[
            pl.BlockSpec((tm, tn), lambda p, i: (i, p)),
            pl.BlockSpec((tm, bw), lambda p, i: (i, p)),
        ],
        out_shape=[jax.ShapeDtypeStruct((n, LRU_BLOCKS * tn), BF16), jax.ShapeDtypeStruct((n, width), BF16)],
        scratch_shapes=[
            pltpu.VMEM((d, tn), BF16),
            pltpu.VMEM((CONV_HALO + tm, bw), F32),
            pltpu.VMEM((SUBLANES, bw), F32),
        ],
        compiler_params=_compiler_params(("arbitrary", "arbitrary"), vmem),
        name="in_proj_lru",
    )(h, w_stack, proj_a, proj_a, conv_w, vec(conv_b), w_ax, vec(b_a), vec(b_x), vec(lam))


def _mem_attn_kernel(q_ref, kv_ref, gate_ref, o_ref):
    scale = MEM_HEAD_DIM ** -0.5
    kv_width = MEM_HEADS * MEM_HEAD_DIM
    for hd in range(MEM_HEADS):
        cols = slice(hd * MEM_HEAD_DIM, (hd + 1) * MEM_HEAD_DIM)
        vcols = slice(kv_width + hd * MEM_HEAD_DIM, kv_width + (hd + 1) * MEM_HEAD_DIM)
        s = lax.dot_general(q_ref[:, cols], kv_ref[:, cols], (((1,), (1,)), ((), ())),
                            preferred_element_type=F32) * scale
        p = jnp.exp(s - jnp.max(s, axis=-1, keepdims=True))
        l = jnp.sum(p, axis=-1, keepdims=True)
        o = jnp.dot(p.astype(BF16), kv_ref[:, vcols], preferred_element_type=F32) / l
        o_ref[:, cols] = (o * _silu(gate_ref[:, cols].astype(F32))).astype(o_ref.dtype)


def _memory_attention(proj, mkv, *, seq, mem_len, col_q, col_gate, tq):
    n = proj.shape[0]
    width = MEM_HEADS * MEM_HEAD_DIM
    tiles_per_seq = seq // tq
    bq, bg = _col_block(col_q, width), _col_block(col_gate, width)
    vmem = 2 * (3 * tq * width * 2 + mem_len * 2 * width * 2) + 8 * tq * mem_len * 4 + (4 << 20)
    return pl.pallas_call(
        _mem_attn_kernel,
        grid=(n // tq,),
        in_specs=[
            pl.BlockSpec((tq, width), lambda i: (i, bq)),
            pl.BlockSpec((mem_len, 2 * width), lambda i: (i // tiles_per_seq, 0)),
            pl.BlockSpec((tq, width), lambda i: (i, bg)),
        ],
        out_specs=pl.BlockSpec((tq, width), lambda i: (i, 0)),
        out_shape=jax.ShapeDtypeStruct((n, width), BF16),
        compiler_params=_compiler_params(("arbitrary",), vmem),
        name="memory_attention",
    )(proj, mkv, proj)


def _merge_kernel(*refs):
    nb = (len(refs) - 1) // 3
    y_refs, w_refs, g_refs, o_ref = refs[:nb], refs[nb:2 * nb], refs[2 * nb:3 * nb], refs[-1]
    acc = None
    for y_ref, w_ref, g_ref in zip(y_refs, w_refs, g_refs):
        term = _sigmoid(g_ref[...].astype(F32)) * jnp.dot(y_ref[...], w_ref[...], preferred_element_type=F32)
        acc = term if acc is None else acc + term
    o_ref[...] = acc.astype(o_ref.dtype)


def _merge(branches, proj, w_branch, *, col_merge, tm, tn):
    n = proj.shape[0]
    d = w_branch.shape[1]
    sizes = [y.shape[1] for y in branches]
    offsets = [sum(sizes[:i]) for i in range(len(sizes))]
    y_specs = [pl.BlockSpec((tm, sz), lambda i, j: (i, 0)) for sz in sizes]
    w_specs = [pl.BlockSpec((sz, tn), lambda i, j, r=_col_block(off, sz): (r, j)) for sz, off in zip(sizes, offsets)]
    g_specs = [pl.BlockSpec((tm, tn), lambda i, j, c=_col_block(col_merge + bi * d, tn): (i, c + j))
               for bi in range(len(sizes))]
    total = sum(sizes)
    vmem = 2 * (tm * total * 2 + total * tn * 2 + len(sizes) * tm * tn * 2 + tm * tn * 2) + 4 * tm * tn * 4 + (4 << 20)
    return pl.pallas_call(
        _merge_kernel,
        grid=(n // tm, d // tn),
        in_specs=y_specs + w_specs + g_specs,
        out_specs=pl.BlockSpec((tm, tn), lambda i, j: (i, j)),
        out_shape=jax.ShapeDtypeStruct((n, d), BF16),
        compiler_params=_compiler_params(("arbitrary", "arbitrary"), vmem),
        name="branch_merge",
    )(*branches, *([w_branch] * len(sizes)), *([proj] * len(sizes)))


def _out_proj_kernel(x_ref, m_ref, w_ref, g_ref, *o_refs, last):
    y = x_ref[...] + jnp.dot(m_ref[...], w_ref[...], preferred_element_type=F32)
    normed = _rms(y, g_ref[...])
    if last:
        o_refs[0][...] = normed
    else:
        o_refs[0][...] = y
        o_refs[1][...] = normed.astype(BF16)


def _out_proj(x, merged, w_out, g, *, last, tm):
    n, d = x.shape
    vmem = 2 * (2 * tm * d * 4 + 2 * tm * d * 2 + d * d * 2) + 3 * tm * d * 4 + (4 << 20)
    row_spec = pl.BlockSpec((tm, d), lambda i: (i, 0))
    out_shape = [jax.ShapeDtypeStruct((n, d), F32)] + ([] if last else [jax.ShapeDtypeStruct((n, d), BF16)])
    return pl.pallas_call(
        functools.partial(_out_proj_kernel, last=last),
        grid=(n // tm,),
        in_specs=[row_spec, row_spec, pl.BlockSpec((d, d), lambda i: (0, 0)), pl.BlockSpec((1, d), lambda i: (0, 0))],
        out_specs=[row_spec] * len(out_shape),
        out_shape=out_shape,
        compiler_params=_compiler_params(("arbitrary",), vmem),
        name="out_proj",
    )(x, merged, w_out, g.reshape(1, d))


def kernel(x, mem, norm_g, w_in, lam_vecs, subln_g, pool_w, pool_scale, conv_w, conv_b, lru_wa, lru_ba,
           lru_wx, lru_bx, lru_lambda, mem_norm_g, w_mem_kv, w_branch, w_out, final_g):
    batch, seq, d = x.shape
    mem_len = mem.shape[1]
    depth = w_in.shape[0]
    da_width = DA_HEADS * DA_V_DIM
    pool_width = pool_scale.shape[-1]
    lru_width = conv_b.shape[-1]
    mem_width = MEM_HEADS * MEM_HEAD_DIM
    branch_width = da_width + pool_width + lru_width + mem_width
    col_q = 0
    col_k = col_q + da_width
    col_v = col_k + da_width
    col_pool = col_v + da_width
    col_lru = col_pool + pool_width
    col_qmem = col_lru + lru_width
    col_gate = col_qmem + mem_width
    col_merge = col_gate + branch_width
    gate_a = col_gate
    gate_b = gate_a + da_width
    gate_c = gate_b + pool_width
    gate_m = gate_c + lru_width

    xf = x.reshape(batch * seq, d)
    memf = mem.reshape(batch * mem_len, d)
    h = _norm(xf, norm_g[0], tm=512)
    for l in range(depth):
        last = l == depth - 1
        lambda_init = 0.8 - 0.6 * math.exp(-0.3 * l)
        proj = _proj(h, w_in, l, panels=_col_block(col_merge, PROJ_TN), tm=1024, tn=PROJ_TN, name="in_proj")
        w_ax = jnp.concatenate([lru_wa[l], lru_wx[l]], axis=-1).astype(BF16)
        merge_gates, y_c = _proj_lru(h, w_in, l, proj, conv_w[l], conv_b[l], w_ax, lru_ba[l], lru_bx[l],
                                     lru_lambda[l], seq=seq, panel0=_col_block(col_merge, PROJ_TN),
                                     col_x=col_lru, col_gate=gate_c, tm=1024, tn=PROJ_TN)
        mkv = _norm_proj(memf, mem_norm_g[l], w_mem_kv[l].astype(BF16), tm=batch * mem_len, tn=512, name="mem_kv")
        y_a = _diff_attention(proj, lam_vecs[l], subln_g[l], batch=batch, seq=seq, col_q=col_q, col_k=col_k,
                              col_v=col_v, col_gate=gate_a, lambda_init=lambda_init, tq=512)
        y_b = _multiscale_pool(proj, pool_w[l].astype(BF16), pool_scale[l], seq=seq, col_x=col_pool,
                               col_gate=gate_b, ts=512)
        y_m = _memory_attention(proj, mkv, seq=seq, mem_len=mem_len, col_q=col_qmem, col_gate=gate_m, tq=512)
        merged = _merge([y_a, y_b, y_c, y_m], merge_gates, w_branch[l].astype(BF16), col_merge=0, tm=1024, tn=512)
        if last:
            (out,) = _out_proj(xf, merged, w_out[l].astype(BF16), final_g, last=True, tm=512)
        else:
            xf, h = _out_proj(xf, merged, w_out[l].astype(BF16), norm_g[l + 1], last=False, tm=512)
    return out.reshape(batch, seq, d)
```

```python
import functools
import math

import jax
import jax.numpy as jnp
from jax import lax
from jax.experimental import pallas as pl
from jax.experimental.pallas import tpu as pltpu

F32 = jnp.float32
BF16 = jnp.bfloat16

EPS = 1e-6
DA_HEADS = 8
DA_HEAD_DIM = 64
DA_V_DIM = 2 * DA_HEAD_DIM
POOL_WINDOWS = (2, 4, 8, 16)
POOL_HALO = 32
LRU_BLOCKS = 8
CONV_WIDTH = 4
CONV_HALO = 8
LRU_C = 8.0
MEM_HEADS = 4
MEM_HEAD_DIM = 128
LANES = 128
SUBLANES = 8
V7X_VMEM_BYTES = 64 * 1024 * 1024
PROJ_TN = 1024
MASK_VALUE = -1e30
LOG2E = 1.4426950408889634


def _compiler_params(semantics, vmem_bytes):
    assert vmem_bytes < V7X_VMEM_BYTES
    return pltpu.CompilerParams(dimension_semantics=semantics, vmem_limit_bytes=int(vmem_bytes))


def _col_block(col, width):
    assert col % width == 0, (col, width)
    return col // width


def _rms(x, g):
    ms = jnp.mean(x * x, axis=-1, keepdims=True)
    return x * lax.rsqrt(ms + EPS) * g


def _lane_tile(x, n):
    return jnp.concatenate([x] * n, axis=1)


def _sigmoid(x):
    return 0.5 * jnp.tanh(0.5 * x) + 0.5


def _silu(x):
    return x * _sigmoid(x)


def _norm_proj_kernel(x_ref, g_ref, w_ref, o_ref, h_ref):
    @pl.when(pl.program_id(1) == 0)
    def _():
        h_ref[...] = _rms(x_ref[...], g_ref[...]).astype(BF16)

    o_ref[...] = jnp.dot(h_ref[...], w_ref[...], preferred_element_type=F32).astype(o_ref.dtype)


def _norm_proj(x, g, w, *, tm, tn, name):
    n, d = x.shape
    nc = w.shape[1]
    vmem = 2 * (tm * d * 4 + d * tn * 2 + tm * tn * 2) + tm * d * 2 + (4 << 20)
    return pl.pallas_call(
        _norm_proj_kernel,
        grid=(n // tm, nc // tn),
        in_specs=[
            pl.BlockSpec((tm, d), lambda i, j: (i, 0)),
            pl.BlockSpec((1, d), lambda i, j: (0, 0)),
            pl.BlockSpec((d, tn), lambda i, j: (0, j)),
        ],
        out_specs=pl.BlockSpec((tm, tn), lambda i, j: (i, j)),
        out_shape=jax.ShapeDtypeStruct((n, nc), BF16),
        scratch_shapes=[pltpu.VMEM((tm, d), BF16)],
        compiler_params=_compiler_params(("arbitrary", "arbitrary"), vmem),
        name=name,
    )(x, g.reshape(1, d), w)


def _norm_kernel(x_ref, g_ref, o_ref):
    o_ref[...] = _rms(x_ref[...], g_ref[...]).astype(o_ref.dtype)


def _norm(x, g, *, tm):
    n, d = x.shape
    vmem = 2 * (tm * d * 4 + tm * d * 2) + 3 * tm * d * 4 + (4 << 20)
    return pl.pallas_call(
        _norm_kernel,
        grid=(n // tm,),
        in_specs=[pl.BlockSpec((tm, d), lambda i: (i, 0)), pl.BlockSpec((1, d), lambda i: (0, 0))],
        out_specs=pl.BlockSpec((tm, d), lambda i: (i, 0)),
        out_shape=jax.ShapeDtypeStruct((n, d), BF16),
        compiler_params=_compiler_params(("arbitrary",), vmem),
        name="rms_norm",
    )(x, g.reshape(1, d))


def _proj_kernel(h_ref, w_ref, o_ref, wb_ref):
    @pl.when(pl.program_id(1) == 0)
    def _():
        wb_ref[...] = w_ref[...].astype(BF16)

    o_ref[...] = jnp.dot(h_ref[...], wb_ref[...], preferred_element_type=F32).astype(o_ref.dtype)


def _proj(h, w_stack, layer, *, panels, tm, tn, name):
    n, d = h.shape
    vmem = 2 * (tm * d * 2 + d * tn * 4 + tm * tn * 2) + d * tn * 2 + tm * tn * 4 + (4 << 20)
    return pl.pallas_call(
        _proj_kernel,
        grid=(panels, n // tm),
        in_specs=[
            pl.BlockSpec((tm, d), lambda j, i: (i, 0)),
            pl.BlockSpec((None, d, tn), lambda j, i: (layer, 0, j)),
        ],
        out_specs=pl.BlockSpec((tm, tn), lambda j, i: (i, j)),
        out_shape=jax.ShapeDtypeStruct((n, panels * tn), BF16),
        scratch_shapes=[pltpu.VMEM((d, tn), BF16)],
        compiler_params=_compiler_params(("arbitrary", "arbitrary"), vmem),
        name=name,
    )(h, w_stack)


def _diff_attn_kernel(q_ref, k_ref, v_ref, gate_ref, lamv_ref, subg_ref, o_ref,
                      qs_ref, vx_ref, m_ref, acc_ref, sa_ref, sb_ref, *, tq, lambda_init):
    g = pl.program_id(2)
    rows = 2 * tq
    hw = DA_V_DIM
    scale2 = (DA_HEAD_DIM ** -0.5) * LOG2E

    @pl.when(g == 0)
    def _():
        vx_ref[:, 0:hw] = v_ref[...]
        vx_ref[:, hw:2 * hw] = jnp.ones(v_ref.shape, v_ref.dtype)

    for t in range(2):
        q = q_ref[t * tq:(t + 1) * tq, :]
        lane = lax.broadcasted_iota(jnp.int32, q.shape, 1)
        zero = jnp.zeros_like(q)
        qs_ref[t * rows:t * rows + tq, :] = jnp.where(lane < DA_HEAD_DIM, q, zero)
        qs_ref[t * rows + tq:(t + 1) * rows, :] = jnp.where(lane >= DA_HEAD_DIM, q, zero)
    m_ref[...] = jnp.full(m_ref.shape, MASK_VALUE, F32)
    acc_ref[...] = jnp.zeros(acc_ref.shape, F32)

    def scores(j, s_ref, row0, nrows):
        k = k_ref[pl.ds(pl.multiple_of(j * tq, tq), tq), :]
        s_ref[0:nrows, :] = lax.dot_general(qs_ref[row0:row0 + nrows, :], k, (((1,), (1,)), ((), ())),
                                            preferred_element_type=F32)

    def update(j, s_ref, row0, nrows, diag_tile):
        s = s_ref[0:nrows, :]
        if diag_tile is not None:
            row = lax.broadcasted_iota(jnp.int32, s.shape, 0) + row0
            col = lax.broadcasted_iota(jnp.int32, s.shape, 1)
            in_tile = row - jnp.where(row >= rows, rows, 0)
            q_pos = jnp.where(in_tile >= tq, in_tile - tq, in_tile)
            if diag_tile == 0:
                q_pos = q_pos + jnp.where(row >= rows, tq, 0)
            s = jnp.where(col <= q_pos, s, MASK_VALUE)
        m_prev = m_ref[row0:row0 + nrows, :]
        m_next = jnp.maximum(m_prev, jnp.max(s, axis=-1, keepdims=True))
        p = jnp.exp2((s - _lane_tile(m_next, tq // LANES)) * scale2)
        alpha = jnp.exp2((m_prev - m_next) * scale2)
        vx = vx_ref[pl.ds(pl.multiple_of(j * tq, tq), tq), :]
        pv = jnp.dot(p.astype(BF16), vx, preferred_element_type=F32)
        acc_ref[row0:row0 + nrows, :] = _lane_tile(alpha, 2) * acc_ref[row0:row0 + nrows, :] + pv
        m_ref[row0:row0 + nrows, :] = m_next

    lv = lamv_ref[...]
    lam = (jnp.exp(jnp.sum(lv[0:1, :] * lv[1:2, :], axis=-1, keepdims=True))
           - jnp.exp(jnp.sum(lv[2:3, :] * lv[3:4, :], axis=-1, keepdims=True)) + lambda_init)

    def finalize(t):
        o_maps = acc_ref[t * rows:(t + 1) * rows, 0:hw] / acc_ref[t * rows:(t + 1) * rows, hw:2 * hw]
        o = o_maps[0:tq, :] - lam * o_maps[tq:rows, :]
        y = _rms(o, subg_ref[...]) * (1.0 - lambda_init)
        out_rows = slice(t * tq, (t + 1) * tq)
        o_ref[out_rows, :] = (y * _silu(gate_ref[out_rows, :].astype(F32))).astype(o_ref.dtype)

    scores(0, sa_ref, 0, 2 * rows)

    def unmasked_pair(jj, carry):
        scores(2 * jj + 1, sb_ref, 0, 2 * rows)
        update(2 * jj, sa_ref, 0, 2 * rows, None)
        scores(2 * jj + 2, sa_ref, 0, 2 * rows)
        update(2 * jj + 1, sb_ref, 0, 2 * rows, None)
        return carry

    lax.fori_loop(0, g, unmasked_pair, 0)
    scores(2 * g + 1, sb_ref, rows, rows)
    update(2 * g, sa_ref, 0, 2 * rows, 0)
    finalize(0)
    update(2 * g + 1, sb_ref, rows, rows, 1)
    finalize(1)


def _diff_attention(proj, lam_vecs, subln_g, *, batch, seq, col_q, col_k, col_v, col_gate, lambda_init, tq):
    n = batch * seq
    assert seq % (2 * tq) == 0
    ng = seq // (2 * tq)
    hw = DA_V_DIM
    bq, bk, bv, bg = (_col_block(c, hw) for c in (col_q, col_k, col_v, col_gate))
    vmem = (2 * (3 * 2 * tq * hw * 2 + 2 * seq * hw * 2) + 2 * 2 * tq * hw * (2 + 12) + seq * 2 * hw * 2
            + 6 * 4 * tq * tq * 4 + (4 << 20))
    kernel = functools.partial(_diff_attn_kernel, tq=tq, lambda_init=lambda_init)
    return pl.pallas_call(
        kernel,
        grid=(batch, DA_HEADS, ng),
        in_specs=[
            pl.BlockSpec((2 * tq, hw), lambda b, h, i: (b * ng + i, bq + h)),
            pl.BlockSpec((seq, hw), lambda b, h, i: (b, bk + h)),
            pl.BlockSpec((seq, hw), lambda b, h, i: (b, bv + h)),
            pl.BlockSpec((2 * tq, hw), lambda b, h, i: (b * ng + i, bg + h)),
            pl.BlockSpec(lam_vecs.shape, lambda b, h, i: (0, 0)),
            pl.BlockSpec((1, hw), lambda b, h, i: (0, 0)),
        ],
        out_specs=pl.BlockSpec((2 * tq, hw), lambda b, h, i: (b * ng + i, h)),
        out_shape=jax.ShapeDtypeStruct((n, DA_HEADS * hw), BF16),
        scratch_shapes=[
            pltpu.VMEM((4 * tq, hw), BF16),
            pltpu.VMEM((seq, 2 * hw), BF16),
            pltpu.VMEM((4 * tq, LANES), F32),
            pltpu.VMEM((4 * tq, 2 * hw), F32),
            pltpu.VMEM((4 * tq, tq), F32),
            pltpu.VMEM((4 * tq, tq), F32),
        ],
        compiler_params=_compiler_params(("arbitrary", "arbitrary", "arbitrary"), vmem),
        name="diff_attention",
    )(proj, proj, proj, proj, lam_vecs, subln_g.reshape(1, hw))


def _gate_cols(gate_refs, cols):
    piece = gate_refs[0].shape[1]
    idx, start = divmod(cols.start, piece)
    assert cols.stop - cols.start <= piece - start
    return _silu(gate_refs[idx][:, start:start + cols.stop - cols.start].astype(F32))


def _pool_kernel(x_ref, halo_ref, gate_lo_ref, gate_hi_ref, w_ref, scale_ref, o_ref, xe_ref, la_ref, lb_ref,
                 *, ts, tiles_per_seq):
    gate_refs = (gate_lo_ref, gate_hi_ref)
    t = pl.program_id(0) % tiles_per_seq
    halo = halo_ref[...].astype(F32)
    xe_ref[0:POOL_HALO, :] = jnp.where(t == 0, jnp.zeros_like(halo), halo)
    x = x_ref[...].astype(F32)
    xe_ref[POOL_HALO:POOL_HALO + ts, :] = x
    gw = x.shape[1] // len(POOL_WINDOWS)
    pos1 = t * ts + lax.broadcasted_iota(jnp.int32, (ts, gw), 0) + 1
    for gi, w in enumerate(POOL_WINDOWS):
        cols = slice(gi * gw, (gi + 1) * gw)
        xg = x[:, cols]
        levels = w.bit_length() - 1
        src_ref, src_cols, spare = xe_ref, cols, [la_ref, lb_ref]
        for lv in range(levels):
            shift = 1 << lv
            lo = POOL_HALO if lv == levels - 1 else SUBLANES * (lv + 1)
            nrows = POOL_HALO + ts - lo
            wsum = src_ref[lo:lo + nrows, src_cols] + src_ref[lo - shift:lo - shift + nrows, src_cols]
            if lv < levels - 1:
                dst_ref = spare[lv % 2]
                dst_ref[lo:lo + nrows, :] = wsum
                src_ref, src_cols = dst_ref, slice(0, gw)
        count = jnp.minimum(pos1, w).astype(F32)
        d = (wsum / count - xg).astype(BF16)
        y = jnp.dot(d, w_ref[gi], preferred_element_type=F32) * scale_ref[:, cols]
        o_ref[:, cols] = (y * _gate_cols(gate_refs, cols)).astype(o_ref.dtype)


def _multiscale_pool(proj, pool_w, pool_scale, *, seq, col_x, col_gate, ts):
    n = proj.shape[0]
    width = pool_scale.shape[-1]
    tiles_per_seq = seq // ts
    halo_per_tile = ts // POOL_HALO
    half = width // 2
    bx, bg = _col_block(col_x, width), _col_block(col_gate, half)
    vmem = 2 * (3 * ts * width * 2 + POOL_HALO * width * 2 + pool_w.size * 2) + 8 * ts * width * 4 + (4 << 20)
    kernel = functools.partial(_pool_kernel, ts=ts, tiles_per_seq=tiles_per_seq)
    return pl.pallas_call(
        kernel,
        grid=(n // ts,),
        in_specs=[
            pl.BlockSpec((ts, width), lambda i: (i, bx)),
            pl.BlockSpec((POOL_HALO, width), lambda i: (jnp.maximum(i * halo_per_tile - 1, 0), bx)),
            pl.BlockSpec((ts, half), lambda i: (i, bg)),
            pl.BlockSpec((ts, half), lambda i: (i, bg + 1)),
            pl.BlockSpec(pool_w.shape, lambda i: (0, 0, 0)),
            pl.BlockSpec((1, width), lambda i: (0, 0)),
        ],
        out_specs=pl.BlockSpec((ts, width), lambda i: (i, 0)),
        out_shape=jax.ShapeDtypeStruct((n, width), BF16),
        scratch_shapes=[pltpu.VMEM((POOL_HALO + ts, width), F32)]
        + [pltpu.VMEM((POOL_HALO + ts, width // len(POOL_WINDOWS)), F32)] * 2,
        compiler_params=_compiler_params(("arbitrary",), vmem),
        name="multiscale_pool",
    )(proj, proj, proj, proj, pool_w, pool_scale.reshape(1, width))


def _proj_lru_kernel(h_ref, w_ref, x_ref, gate_ref, cw_ref, cb_ref, wax_ref, ba_ref, bx_ref, lam_ref,
                     o_ref, y_ref, wb_ref, xe_ref, carry_ref, *, tiles_per_seq):
    i = pl.program_id(1)
    t = i % tiles_per_seq
    ts, bw = x_ref.shape

    @pl.when(i == 0)
    def _():
        wb_ref[...] = w_ref[...].astype(BF16)

    @pl.when(t == 0)
    def _():
        xe_ref[0:CONV_HALO, :] = jnp.zeros((CONV_HALO, bw), F32)
        carry_ref[...] = jnp.zeros(carry_ref.shape, F32)

    @pl.when(t > 0)
    def _():
        xe_ref[0:CONV_HALO, :] = xe_ref[ts:ts + CONV_HALO, :]

    xe_ref[CONV_HALO:CONV_HALO + ts, :] = x_ref[...].astype(F32)

    xc = cb_ref[...] + cw_ref[CONV_WIDTH - 1:CONV_WIDTH, :] * xe_ref[CONV_HALO:CONV_HALO + ts, :]
    for back in range(1, CONV_WIDTH):
        tap = CONV_WIDTH - 1 - back
        xc = xc + cw_ref[tap:tap + 1, :] * xe_ref[CONV_HALO - back:CONV_HALO - back + ts, :]

    neg_lam = -lam_ref[...]
    softplus = jnp.maximum(neg_lam, 0.0) + jnp.log(1.0 + jnp.exp(-jnp.abs(neg_lam)))
    z = jnp.dot(xc.astype(BF16), wax_ref[...], preferred_element_type=F32)
    o_ref[...] = jnp.dot(h_ref[...], wb_ref[...], preferred_element_type=F32).astype(o_ref.dtype)

    r = _sigmoid(z[:, 0:bw] + ba_ref[...])
    gate_i = _sigmoid(z[:, bw:2 * bw] + bx_ref[...])
    log_a = (-LRU_C) * r * softplus
    th = jnp.tanh(log_a)
    m2 = -2.0 * th / (1.0 - th)
    mult = jnp.where(m2 > 0.0, m2 * lax.rsqrt(m2), 0.0)
    row = lax.broadcasted_iota(jnp.int32, (ts, bw), 0)
    first_row = jnp.where(t == 0, 0, -1)
    mult = jnp.where(row == first_row, 1.0, mult)
    a_all = jnp.exp(log_a)
    u_all = mult * (gate_i * xc)

    sub = lax.broadcasted_iota(jnp.int32, (SUBLANES, bw), 0)
    carry = carry_ref[...]
    hs = []
    for gidx in range(ts // SUBLANES):
        rows = slice(gidx * SUBLANES, (gidx + 1) * SUBLANES)
        a, b = a_all[rows, :], u_all[rows, :]
        for k in (1, 2, 4):
            keep = sub >= k
            a_prev = jnp.where(keep, pltpu.roll(a, k, 0), 1.0)
            b_prev = jnp.where(keep, pltpu.roll(b, k, 0), 0.0)
            b = a * b_prev + b
            a = a * a_prev
        h = a * carry + b
        hs.append(h)
        carry = jnp.broadcast_to(h[SUBLANES - 1:SUBLANES, :], h.shape)
    carry_ref[...] = carry
    y_ref[...] = (jnp.concatenate(hs, axis=0) * _silu(gate_ref[...].astype(F32))).astype(y_ref.dtype)


def _proj_lru(h, w_stack, layer, proj_a, conv_w, conv_b, w_ax, b_a, b_x, lam, *, seq, panel0, col_x, col_gate,
              tm, tn):
    n, d = h.shape
    assert w_stack.shape[2] == (panel0 + LRU_BLOCKS) * tn and seq % tm == 0
    width = conv_b.shape[-1]
    bw = width // LRU_BLOCKS
    bx, bg = _col_block(col_x, bw), _col_block(col_gate, bw)
    vec = lambda a: a.reshape(1, width)
    vmem = (2 * (tm * d * 2 + d * tn * 4 + tm * tn * 2) + d * tn * 2 + tm * tn * 4
            + 24 * tm * bw * 4 + (4 << 20))
    kernel = functools.partial(_proj_lru_kernel, tiles_per_seq=seq // tm)
    vec_spec = pl.BlockSpec((1, bw), lambda p, i: (0, p))
    return pl.pallas_call(
        kernel,
        grid=(LRU_BLOCKS, n // tm),
        in_specs=[
            pl.BlockSpec((tm, d), lambda p, i: (i, 0)),
            pl.BlockSpec((None, d, tn), lambda p, i: (layer, 0, panel0 + p)),
            pl.BlockSpec((tm, bw), lambda p, i: (i, bx + p)),
            pl.BlockSpec((tm, bw), lambda p, i: (i, bg + p)),
            pl.BlockSpec((CONV_WIDTH, bw), lambda p, i: (0, p)),
            vec_spec,
            pl.BlockSpec((None, bw, 2 * bw), lambda p, i: (p, 0, 0)),
            vec_spec, vec_spec, vec_spec,
        ],
        out_specs=[
            pl.BlockSpec((tm, tn), lambda p, i: (i, p)),
            pl.BlockSpec((tm, bw), lambda p, i: (i, p)),
        ],
        out_shape=[jax.ShapeDtypeStruct((n, LRU_BLOCKS * tn), BF16), jax.ShapeDtypeStruct((n, width), BF16)],
        scratch_shapes=[
            pltpu.VMEM((d, tn), BF16),
            pltpu.VMEM((CONV_HALO + tm, bw), F32),
            pltpu.VMEM((SUBLANES, bw), F32),
        ],
        compiler_params=_compiler_params(("arbitrary", "arbitrary"), vmem),
        name="in_proj_lru",
    )(h, w_stack, proj_a, proj_a, conv_w, vec(conv_b), w_ax, vec(b_a), vec(b_x), vec(lam))


def _mem_attn_kernel(q_ref, kv_ref, gate_ref, o_ref):
    scale = MEM_HEAD_DIM ** -0.5
    kv_width = MEM_HEADS * MEM_HEAD_DIM
    for hd in range(MEM_HEADS):
        cols = slice(hd * MEM_HEAD_DIM, (hd + 1) * MEM_HEAD_DIM)
        vcols = slice(kv_width + hd * MEM_HEAD_DIM, kv_width + (hd + 1) * MEM_HEAD_DIM)
        s = lax.dot_general(q_ref[:, cols], kv_ref[:, cols], (((1,), (1,)), ((), ())),
                            preferred_element_type=F32) * scale
        p = jnp.exp(s - jnp.max(s, axis=-1, keepdims=True))
        l = jnp.sum(p, axis=-1, keepdims=True)
        o = jnp.dot(p.astype(BF16), kv_ref[:, vcols], preferred_element_type=F32) / l
        o_ref[:, cols] = (o * _silu(gate_ref[:, cols].astype(F32))).astype(o_ref.dtype)


def _memory_attention(proj, mkv, *, seq, mem_len, col_q, col_gate, tq):
    n = proj.shape[0]
    width = MEM_HEADS * MEM_HEAD_DIM
    tiles_per_seq = seq // tq
    bq, bg = _col_block(col_q, width), _col_block(col_gate, width)
    vmem = 2 * (3 * tq * width * 2 + mem_len * 2 * width * 2) + 8 * tq * mem_len * 4 + (4 << 20)
    return pl.pallas_call(
        _mem_attn_kernel,
        grid=(n // tq,),
        in_specs=[
            pl.BlockSpec((tq, width), lambda i: (i, bq)),
            pl.BlockSpec((mem_len, 2 * width), lambda i: (i // tiles_per_seq, 0)),
            pl.BlockSpec((tq, width), lambda i: (i, bg)),
        ],
        out_specs=pl.BlockSpec((tq, width), lambda i: (i, 0)),
        out_shape=jax.ShapeDtypeStruct((n, width), BF16),
        compiler_params=_compiler_params(("arbitrary",), vmem),
        name="memory_attention",
    )(proj, mkv, proj)


def _merge_kernel(*refs):
    nb = (len(refs) - 1) // 3
    y_refs, w_refs, g_refs, o_ref = refs[:nb], refs[nb:2 * nb], refs[2 * nb:3 * nb], refs[-1]
    acc = None
    for y_ref, w_ref, g_ref in zip(y_refs, w_refs, g_refs):
        term = _sigmoid(g_ref[...].astype(F32)) * jnp.dot(y_ref[...], w_ref[...], preferred_element_type=F32)
        acc = term if acc is None else acc + term
    o_ref[...] = acc.astype(o_ref.dtype)


def _merge(branches, proj, w_branch, *, col_merge, tm, tn):
    n = proj.shape[0]
    d = w_branch.shape[1]
    sizes = [y.shape[1] for y in branches]
    offsets = [sum(sizes[:i]) for i in range(len(sizes))]
    y_specs = [pl.BlockSpec((tm, sz), lambda i, j: (i, 0)) for sz in sizes]
    w_specs = [pl.BlockSpec((sz, tn), lambda i, j, r=_col_block(off, sz): (r, j)) for sz, off in zip(sizes, offsets)]
    g_specs = [pl.BlockSpec((tm, tn), lambda i, j, c=_col_block(col_merge + bi * d, tn): (i, c + j))
               for bi in range(len(sizes))]
    total = sum(sizes)
    vmem = 2 * (tm * total * 2 + total * tn * 2 + len(sizes) * tm * tn * 2 + tm * tn * 2) + 4 * tm * tn * 4 + (4 << 20)
    return pl.pallas_call(
        _merge_kernel,
        grid=(n // tm, d // tn),
        in_specs=y_specs + w_specs + g_specs,
        out_specs=pl.BlockSpec((tm, tn), lambda i, j: (i, j)),
        out_shape=jax.ShapeDtypeStruct((n, d), BF16),
        compiler_params=_compiler_params(("arbitrary", "arbitrary"), vmem),
        name="branch_merge",
    )(*branches, *([w_branch] * len(sizes)), *([proj] * len(sizes)))


def _out_proj_kernel(x_ref, m_ref, w_ref, g_ref, *o_refs, last):
    y = x_ref[...] + jnp.dot(m_ref[...], w_ref[...], preferred_element_type=F32)
    normed = _rms(y, g_ref[...])
    if last:
        o_refs[0][...] = normed
    else:
        o_refs[0][...] = y
        o_refs[1][...] = normed.astype(BF16)


def _out_proj(x, merged, w_out, g, *, last, tm):
    n, d = x.shape
    vmem = 2 * (2 * tm * d * 4 + 2 * tm * d * 2 + d * d * 2) + 3 * tm * d * 4 + (4 << 20)
    row_spec = pl.BlockSpec((tm, d), lambda i: (i, 0))
    out_shape = [jax.ShapeDtypeStruct((n, d), F32)] + ([] if last else [jax.ShapeDtypeStruct((n, d), BF16)])
    return pl.pallas_call(
        functools.partial(_out_proj_kernel, last=last),
        grid=(n // tm,),
        in_specs=[row_spec, row_spec, pl.BlockSpec((d, d), lambda i: (0, 0)), pl.BlockSpec((1, d), lambda i: (0, 0))],
        out_specs=[row_spec] * len(out_shape),
        out_shape=out_shape,
        compiler_params=_compiler_params(("arbitrary",), vmem),
        name="out_proj",
    )(x, merged, w_out, g.reshape(1, d))


def kernel(x, mem, norm_g, w_in, lam_vecs, subln_g, pool_w, pool_scale, conv_w, conv_b, lru_wa, lru_ba,
           lru_wx, lru_bx, lru_lambda, mem_norm_g, w_mem_kv, w_branch, w_out, final_g):
    batch, seq, d = x.shape
    mem_len = mem.shape[1]
    depth = w_in.shape[0]
    da_width = DA_HEADS * DA_V_DIM
    pool_width = pool_scale.shape[-1]
    lru_width = conv_b.shape[-1]
    mem_width = MEM_HEADS * MEM_HEAD_DIM
    branch_width = da_width + pool_width + lru_width + mem_width
    col_q = 0
    col_k = col_q + da_width
    col_v = col_k + da_width
    col_pool = col_v + da_width
    col_lru = col_pool + pool_width
    col_qmem = col_lru + lru_width
    col_gate = col_qmem + mem_width
    col_merge = col_gate + branch_width
    gate_a = col_gate
    gate_b = gate_a + da_width
    gate_c = gate_b + pool_width
    gate_m = gate_c + lru_width

    xf = x.reshape(batch * seq, d)
    memf = mem.reshape(batch * mem_len, d)
    h = _norm(xf, norm_g[0], tm=512)
    for l in range(depth):
        last = l == depth - 1
        lambda_init = 0.8 - 0.6 * math.exp(-0.3 * l)
        proj = _proj(h, w_in, l, panels=_col_block(col_merge, PROJ_TN), tm=2048, tn=PROJ_TN, name="in_proj")
        w_ax = jnp.concatenate([lru_wa[l], lru_wx[l]], axis=-1).astype(BF16)
        merge_gates, y_c = _proj_lru(h, w_in, l, proj, conv_w[l], conv_b[l], w_ax, lru_ba[l], lru_bx[l],
                                     lru_lambda[l], seq=seq, panel0=_col_block(col_merge, PROJ_TN),
                                     col_x=col_lru, col_gate=gate_c, tm=1024, tn=PROJ_TN)
        mkv = _norm_proj(memf, mem_norm_g[l], w_mem_kv[l].astype(BF16), tm=batch * mem_len, tn=512, name="mem_kv")
        y_a = _diff_attention(proj, lam_vecs[l], subln_g[l], batch=batch, seq=seq, col_q=col_q, col_k=col_k,
                              col_v=col_v, col_gate=gate_a, lambda_init=lambda_init, tq=512)
        y_b = _multiscale_pool(proj, pool_w[l].astype(BF16), pool_scale[l], seq=seq, col_x=col_pool,
                               col_gate=gate_b, ts=512)
        y_m = _memory_attention(proj, mkv, seq=seq, mem_len=mem_len, col_q=col_qmem, col_gate=gate_m, tq=512)
        merged = _merge([y_a, y_b, y_c, y_m], merge_gates, w_branch[l].astype(BF16), col_merge=0, tm=1024, tn=512)
        if last:
            (out,) = _out_proj(xf, merged, w_out[l].astype(BF16), final_g, last=True, tm=512)
        else:
            xf, h = _out_proj(xf, merged, w_out[l].astype(BF16), norm_g[l + 1], last=False, tm=512)
    return out.reshape(batch, seq, d)
```

```python
import functools
import math

import jax
import jax.numpy as jnp
from jax import lax
from jax.experimental import pallas as pl
from jax.experimental.pallas import tpu as pltpu

F32 = jnp.float32
BF16 = jnp.bfloat16

EPS = 1e-6
DA_HEADS = 8
DA_HEAD_DIM = 64
DA_V_DIM = 2 * DA_HEAD_DIM
POOL_WINDOWS = (2, 4, 8, 16)
POOL_HALO = 32
LRU_BLOCKS = 8
CONV_WIDTH = 4
CONV_HALO = 8
LRU_C = 8.0
MEM_HEADS = 4
MEM_HEAD_DIM = 128
LANES = 128
SUBLANES = 8
V7X_VMEM_BYTES = 64 * 1024 * 1024
PROJ_TN = 1024
MASK_VALUE = -1e30
LOG2E = 1.4426950408889634


def _compiler_params(semantics, vmem_bytes):
    assert vmem_bytes < V7X_VMEM_BYTES
    return pltpu.CompilerParams(dimension_semantics=semantics, vmem_limit_bytes=int(vmem_bytes))


def _col_block(col, width):
    assert col % width == 0, (col, width)
    return col // width


def _rms(x, g):
    ms = jnp.mean(x * x, axis=-1, keepdims=True)
    return x * lax.rsqrt(ms + EPS) * g


def _lane_tile(x, n):
    return jnp.concatenate([x] * n, axis=1)


def _sigmoid(x):
    return 0.5 * jnp.tanh(0.5 * x) + 0.5


def _silu(x):
    return x * _sigmoid(x)


def _norm_proj_kernel(x_ref, g_ref, w_ref, o_ref, h_ref):
    @pl.when(pl.program_id(1) == 0)
    def _():
        h_ref[...] = _rms(x_ref[...], g_ref[...]).astype(BF16)

    o_ref[...] = jnp.dot(h_ref[...], w_ref[...], preferred_element_type=F32).astype(o_ref.dtype)


def _norm_proj(x, g, w, *, tm, tn, name):
    n, d = x.shape
    nc = w.shape[1]
    vmem = 2 * (tm * d * 4 + d * tn * 2 + tm * tn * 2) + tm * d * 2 + (4 << 20)
    return pl.pallas_call(
        _norm_proj_kernel,
        grid=(n // tm, nc // tn),
        in_specs=[
            pl.BlockSpec((tm, d), lambda i, j: (i, 0)),
            pl.BlockSpec((1, d), lambda i, j: (0, 0)),
            pl.BlockSpec((d, tn), lambda i, j: (0, j)),
        ],
        out_specs=pl.BlockSpec((tm, tn), lambda i, j: (i, j)),
        out_shape=jax.ShapeDtypeStruct((n, nc), BF16),
        scratch_shapes=[pltpu.VMEM((tm, d), BF16)],
        compiler_params=_compiler_params(("arbitrary", "arbitrary"), vmem),
        name=name,
    )(x, g.reshape(1, d), w)


def _norm_kernel(x_ref, g_ref, o_ref):
    o_ref[...] = _rms(x_ref[...], g_ref[...]).astype(o_ref.dtype)


def _norm(x, g, *, tm):
    n, d = x.shape
    vmem = 2 * (tm * d * 4 + tm * d * 2) + 3 * tm * d * 4 + (4 << 20)
    return pl.pallas_call(
        _norm_kernel,
        grid=(n // tm,),
        in_specs=[pl.BlockSpec((tm, d), lambda i: (i, 0)), pl.BlockSpec((1, d), lambda i: (0, 0))],
        out_specs=pl.BlockSpec((tm, d), lambda i: (i, 0)),
        out_shape=jax.ShapeDtypeStruct((n, d), BF16),
        compiler_params=_compiler_params(("arbitrary",), vmem),
        name="rms_norm",
    )(x, g.reshape(1, d))


def _proj_kernel(h_ref, w_ref, o_ref, wb_ref):
    @pl.when(pl.program_id(1) == 0)
    def _():
        wb_ref[...] = w_ref[...].astype(BF16)

    o_ref[...] = jnp.dot(h_ref[...], wb_ref[...], preferred_element_type=F32).astype(o_ref.dtype)


def _proj(h, w_stack, layer, *, panels, tm, tn, name):
    n, d = h.shape
    vmem = 2 * (tm * d * 2 + d * tn * 4 + tm * tn * 2) + d * tn * 2 + tm * tn * 4 + (4 << 20)
    return pl.pallas_call(
        _proj_kernel,
        grid=(panels, n // tm),
        in_specs=[
            pl.BlockSpec((tm, d), lambda j, i: (i, 0)),
            pl.BlockSpec((None, d, tn), lambda j, i: (layer, 0, j)),
        ],
        out_specs=pl.BlockSpec((tm, tn), lambda j, i: (i, j)),
        out_shape=jax.ShapeDtypeStruct((n, panels * tn), BF16),
        scratch_shapes=[pltpu.VMEM((d, tn), BF16)],
        compiler_params=_compiler_params(("arbitrary", "arbitrary"), vmem),
        name=name,
    )(h, w_stack)


def _diff_attn_kernel(q_ref, k_ref, v_ref, gate_ref, lamv_ref, subg_ref, o_ref,
                      qs_ref, vx_ref, m_ref, acc_ref, sa_ref, sb_ref, *, tq, lambda_init):
    g = pl.program_id(2)
    rows = 2 * tq
    hw = DA_V_DIM
    scale2 = (DA_HEAD_DIM ** -0.5) * LOG2E

    @pl.when(g == 0)
    def _():
        vx_ref[:, 0:hw] = v_ref[...]
        vx_ref[:, hw:2 * hw] = jnp.ones(v_ref.shape, v_ref.dtype)

    for t in range(2):
        q = q_ref[t * tq:(t + 1) * tq, :]
        lane = lax.broadcasted_iota(jnp.int32, q.shape, 1)
        zero = jnp.zeros_like(q)
        qs_ref[t * rows:t * rows + tq, :] = jnp.where(lane < DA_HEAD_DIM, q, zero)
        qs_ref[t * rows + tq:(t + 1) * rows, :] = jnp.where(lane >= DA_HEAD_DIM, q, zero)
    m_ref[...] = jnp.full(m_ref.shape, MASK_VALUE, F32)
    acc_ref[...] = jnp.zeros(acc_ref.shape, F32)

    def scores(j, s_ref, row0, nrows):
        k = k_ref[pl.ds(pl.multiple_of(j * tq, tq), tq), :]
        s_ref[0:nrows, :] = lax.dot_general(qs_ref[row0:row0 + nrows, :], k, (((1,), (1,)), ((), ())),
                                            preferred_element_type=F32)

    def update(j, s_ref, row0, nrows, diag_tile):
        s = s_ref[0:nrows, :]
        if diag_tile is not None:
            assert row0 == diag_tile * rows
            causal = (lax.broadcasted_iota(jnp.int32, (tq, tq), 1)
                      <= lax.broadcasted_iota(jnp.int32, (tq, tq), 0))
            pieces = [jnp.where(causal, s[k * tq:(k + 1) * tq, :], MASK_VALUE) for k in range(2)]
            s = jnp.concatenate(pieces + ([s[rows:nrows, :]] if nrows > rows else []), axis=0)
        m_prev = m_ref[row0:row0 + nrows, :]
        m_next = jnp.maximum(m_prev, jnp.max(s, axis=-1, keepdims=True))
        p = jnp.exp2((s - _lane_tile(m_next, tq // LANES)) * scale2)
        alpha = jnp.exp2((m_prev - m_next) * scale2)
        vx = vx_ref[pl.ds(pl.multiple_of(j * tq, tq), tq), :]
        pv = jnp.dot(p.astype(BF16), vx, preferred_element_type=F32)
        acc_ref[row0:row0 + nrows, :] = _lane_tile(alpha, 2) * acc_ref[row0:row0 + nrows, :] + pv
        m_ref[row0:row0 + nrows, :] = m_next

    lv = lamv_ref[...]
    lam = (jnp.exp(jnp.sum(lv[0:1, :] * lv[1:2, :], axis=-1, keepdims=True))
           - jnp.exp(jnp.sum(lv[2:3, :] * lv[3:4, :], axis=-1, keepdims=True)) + lambda_init)

    def finalize(t):
        o_maps = acc_ref[t * rows:(t + 1) * rows, 0:hw] / acc_ref[t * rows:(t + 1) * rows, hw:2 * hw]
        o = o_maps[0:tq, :] - lam * o_maps[tq:rows, :]
        y = _rms(o, subg_ref[...]) * (1.0 - lambda_init)
        out_rows = slice(t * tq, (t + 1) * tq)
        o_ref[out_rows, :] = (y * _silu(gate_ref[out_rows, :].astype(F32))).astype(o_ref.dtype)

    scores(0, sa_ref, 0, 2 * rows)

    def unmasked_pair(jj, carry):
        scores(2 * jj + 1, sb_ref, 0, 2 * rows)
        update(2 * jj, sa_ref, 0, 2 * rows, None)
        scores(2 * jj + 2, sa_ref, 0, 2 * rows)
        update(2 * jj + 1, sb_ref, 0, 2 * rows, None)
        return carry

    lax.fori_loop(0, g, unmasked_pair, 0)
    scores(2 * g + 1, sb_ref, rows, rows)
    update(2 * g, sa_ref, 0, 2 * rows, 0)
    finalize(0)
    update(2 * g + 1, sb_ref, rows, rows, 1)
    finalize(1)


def _diff_attention(proj, lam_vecs, subln_g, *, batch, seq, col_q, col_k, col_v, col_gate, lambda_init, tq):
    n = batch * seq
    assert seq % (2 * tq) == 0
    ng = seq // (2 * tq)
    hw = DA_V_DIM
    bq, bk, bv, bg = (_col_block(c, hw) for c in (col_q, col_k, col_v, col_gate))
    vmem = (2 * (3 * 2 * tq * hw * 2 + 2 * seq * hw * 2) + 2 * 2 * tq * hw * (2 + 12) + seq * 2 * hw * 2
            + 6 * 4 * tq * tq * 4 + (4 << 20))
    kernel = functools.partial(_diff_attn_kernel, tq=tq, lambda_init=lambda_init)
    return pl.pallas_call(
        kernel,
        grid=(batch, DA_HEADS, ng),
        in_specs=[
            pl.BlockSpec((2 * tq, hw), lambda b, h, i: (b * ng + i, bq + h)),
            pl.BlockSpec((seq, hw), lambda b, h, i: (b, bk + h)),
            pl.BlockSpec((seq, hw), lambda b, h, i: (b, bv + h)),
            pl.BlockSpec((2 * tq, hw), lambda b, h, i: (b * ng + i, bg + h)),
            pl.BlockSpec(lam_vecs.shape, lambda b, h, i: (0, 0)),
            pl.BlockSpec((1, hw), lambda b, h, i: (0, 0)),
        ],
        out_specs=pl.BlockSpec((2 * tq, hw), lambda b, h, i: (b * ng + i, h)),
        out_shape=jax.ShapeDtypeStruct((n, DA_HEADS * hw), BF16),
        scratch_shapes=[
            pltpu.VMEM((4 * tq, hw), BF16),
            pltpu.VMEM((seq, 2 * hw), BF16),
            pltpu.VMEM((4 * tq, LANES), F32),
            pltpu.VMEM((4 * tq, 2 * hw), F32),
            pltpu.VMEM((4 * tq, tq), F32),
            pltpu.VMEM((4 * tq, tq), F32),
        ],
        compiler_params=_compiler_params(("arbitrary", "arbitrary", "arbitrary"), vmem),
        name="diff_attention",
    )(proj, proj, proj, proj, lam_vecs, subln_g.reshape(1, hw))


def _gate_cols(gate_refs, cols):
    piece = gate_refs[0].shape[1]
    idx, start = divmod(cols.start, piece)
    assert cols.stop - cols.start <= piece - start
    return _silu(gate_refs[idx][:, start:start + cols.stop - cols.start].astype(F32))


def _pool_kernel(x_ref, halo_ref, gate_lo_ref, gate_hi_ref, w_ref, scale_ref, o_ref, xe_ref, la_ref, lb_ref,
                 *, ts, tiles_per_seq):
    gate_refs = (gate_lo_ref, gate_hi_ref)
    t = pl.program_id(0) % tiles_per_seq
    halo = halo_ref[...].astype(F32)
    xe_ref[0:POOL_HALO, :] = jnp.where(t == 0, jnp.zeros_like(halo), halo)
    x = x_ref[...].astype(F32)
    xe_ref[POOL_HALO:POOL_HALO + ts, :] = x
    gw = x.shape[1] // len(POOL_WINDOWS)
    pos1 = t * ts + lax.broadcasted_iota(jnp.int32, (ts, gw), 0) + 1
    for gi, w in enumerate(POOL_WINDOWS):
        cols = slice(gi * gw, (gi + 1) * gw)
        xg = x[:, cols]
        levels = w.bit_length() - 1
        src_ref, src_cols, spare = xe_ref, cols, [la_ref, lb_ref]
        for lv in range(levels):
            shift = 1 << lv
            lo = POOL_HALO if lv == levels - 1 else SUBLANES * (lv + 1)
            nrows = POOL_HALO + ts - lo
            wsum = src_ref[lo:lo + nrows, src_cols] + src_ref[lo - shift:lo - shift + nrows, src_cols]
            if lv < levels - 1:
                dst_ref = spare[lv % 2]
                dst_ref[lo:lo + nrows, :] = wsum
                src_ref, src_cols = dst_ref, slice(0, gw)
        count = jnp.minimum(pos1, w).astype(F32)
        d = (wsum / count - xg).astype(BF16)
        y = jnp.dot(d, w_ref[gi], preferred_element_type=F32) * scale_ref[:, cols]
        o_ref[:, cols] = (y * _gate_cols(gate_refs, cols)).astype(o_ref.dtype)


def _multiscale_pool(proj, pool_w, pool_scale, *, seq, col_x, col_gate, ts):
    n = proj.shape[0]
    width = pool_scale.shape[-1]
    tiles_per_seq = seq // ts
    halo_per_tile = ts // POOL_HALO
    half = width // 2
    bx, bg = _col_block(col_x, width), _col_block(col_gate, half)
    vmem = 2 * (3 * ts * width * 2 + POOL_HALO * width * 2 + pool_w.size * 2) + 8 * ts * width * 4 + (4 << 20)
    kernel = functools.partial(_pool_kernel, ts=ts, tiles_per_seq=tiles_per_seq)
    return pl.pallas_call(
        kernel,
        grid=(n // ts,),
        in_specs=[
            pl.BlockSpec((ts, width), lambda i: (i, bx)),
            pl.BlockSpec((POOL_HALO, width), lambda i: (jnp.maximum(i * halo_per_tile - 1, 0), bx)),
            pl.BlockSpec((ts, half), lambda i: (i, bg)),
            pl.BlockSpec((ts, half), lambda i: (i, bg + 1)),
            pl.BlockSpec(pool_w.shape, lambda i: (0, 0, 0)),
            pl.BlockSpec((1, width), lambda i: (0, 0)),
        ],
        out_specs=pl.BlockSpec((ts, width), lambda i: (i, 0)),
        out_shape=jax.ShapeDtypeStruct((n, width), BF16),
        scratch_shapes=[pltpu.VMEM((POOL_HALO + ts, width), F32)]
        + [pltpu.VMEM((POOL_HALO + ts, width // len(POOL_WINDOWS)), F32)] * 2,
        compiler_params=_compiler_params(("arbitrary",), vmem),
        name="multiscale_pool",
    )(proj, proj, proj, proj, pool_w, pool_scale.reshape(1, width))


def _proj_lru_kernel(h_ref, w_ref, x_ref, gate_ref, cw_ref, cb_ref, wax_ref, ba_ref, bx_ref, lam_ref,
                     o_ref, y_ref, wb_ref, xe_ref, carry_ref, *, tiles_per_seq):
    i = pl.program_id(1)
    t = i % tiles_per_seq
    ts, bw = x_ref.shape

    @pl.when(i == 0)
    def _():
        wb_ref[...] = w_ref[...].astype(BF16)

    @pl.when(t == 0)
    def _():
        xe_ref[0:CONV_HALO, :] = jnp.zeros((CONV_HALO, bw), F32)
        carry_ref[...] = jnp.zeros(carry_ref.shape, F32)

    @pl.when(t > 0)
    def _():
        xe_ref[0:CONV_HALO, :] = xe_ref[ts:ts + CONV_HALO, :]

    xe_ref[CONV_HALO:CONV_HALO + ts, :] = x_ref[...].astype(F32)

    xc = cb_ref[...] + cw_ref[CONV_WIDTH - 1:CONV_WIDTH, :] * xe_ref[CONV_HALO:CONV_HALO + ts, :]
    for back in range(1, CONV_WIDTH):
        tap = CONV_WIDTH - 1 - back
        xc = xc + cw_ref[tap:tap + 1, :] * xe_ref[CONV_HALO - back:CONV_HALO - back + ts, :]

    neg_lam = -lam_ref[...]
    softplus = jnp.maximum(neg_lam, 0.0) + jnp.log(1.0 + jnp.exp(-jnp.abs(neg_lam)))
    z = jnp.dot(xc.astype(BF16), wax_ref[...], preferred_element_type=F32)
    o_ref[...] = jnp.dot(h_ref[...], wb_ref[...], preferred_element_type=F32).astype(o_ref.dtype)

    r = _sigmoid(z[:, 0:bw] + ba_ref[...])
    gate_i = _sigmoid(z[:, bw:2 * bw] + bx_ref[...])
    log_a = (-LRU_C) * r * softplus
    th = jnp.tanh(log_a)
    m2 = -2.0 * th / (1.0 - th)
    mult = jnp.where(m2 > 0.0, m2 * lax.rsqrt(m2), 0.0)
    row = lax.broadcasted_iota(jnp.int32, (ts, bw), 0)
    first_row = jnp.where(t == 0, 0, -1)
    mult = jnp.where(row == first_row, 1.0, mult)
    a_all = jnp.exp(log_a)
    u_all = mult * (gate_i * xc)

    sub = lax.broadcasted_iota(jnp.int32, (SUBLANES, bw), 0)
    carry = carry_ref[...]
    hs = []
    for gidx in range(ts // SUBLANES):
        rows = slice(gidx * SUBLANES, (gidx + 1) * SUBLANES)
        a, b = a_all[rows, :], u_all[rows, :]
        for k in (1, 2, 4):
            keep = sub >= k
            a_prev = jnp.where(keep, pltpu.roll(a, k, 0), 1.0)
            b_prev = jnp.where(keep, pltpu.roll(b, k, 0), 0.0)
            b = a * b_prev + b
            a = a * a_prev
        h = a * carry + b
        hs.append(h)
        carry = jnp.broadcast_to(h[SUBLANES - 1:SUBLANES, :], h.shape)
    carry_ref[...] = carry
    y_ref[...] = (jnp.concatenate(hs, axis=0) * _silu(gate_ref[...].astype(F32))).astype(y_ref.dtype)


def _proj_lru(h, w_stack, layer, proj_a, conv_w, conv_b, w_ax, b_a, b_x, lam, *, seq, panel0, col_x, col_gate,
              tm, tn):
    n, d = h.shape
    assert w_stack.shape[2] == (panel0 + LRU_BLOCKS) * tn and seq % tm == 0
    width = conv_b.shape[-1]
    bw = width // LRU_BLOCKS
    bx, bg = _col_block(col_x, bw), _col_block(col_gate, bw)
    vec = lambda a: a.reshape(1, width)
    vmem = (2 * (tm * d * 2 + d * tn * 4 + tm * tn * 2) + d * tn * 2 + tm * tn * 4
            + 24 * tm * bw * 4 + (4 << 20))
    kernel = functools.partial(_proj_lru_kernel, tiles_per_seq=seq // tm)
    vec_spec = pl.BlockSpec((1, bw), lambda p, i: (0, p))
    return pl.pallas_call(
        kernel,
        grid=(LRU_BLOCKS, n // tm),
        in_specs=[
            pl.BlockSpec((tm, d), lambda p, i: (i, 0)),
            pl.BlockSpec((None, d, tn), lambda p, i: (layer, 0, panel0 + p)),
            pl.BlockSpec((tm, bw), lambda p, i: (i, bx + p)),
            pl.BlockSpec((tm, bw), lambda p, i: (i, bg + p)),
            pl.BlockSpec((CONV_WIDTH, bw), lambda p, i: (0, p)),
            vec_spec,
            pl.BlockSpec((None, bw, 2 * bw), lambda p, i: (p, 0, 0)),
            vec_spec, vec_spec, vec_spec,
        ],
        out_specs=[
            pl.BlockSpec((tm, tn), lambda p, i: (i, p)),
            pl.BlockSpec((tm, bw), lambda p, i: (i, p)),
        ],
        out_shape=[jax.ShapeDtypeStruct((n, LRU_BLOCKS * tn), BF16), jax.ShapeDtypeStruct((n, width), BF16)],
        scratch_shapes=[
            pltpu.VMEM((d, tn), BF16),
            pltpu.VMEM((CONV_HALO + tm, bw), F32),
            pltpu.VMEM((SUBLANES, bw), F32),
        ],
        compiler_params=_compiler_params(("arbitrary", "arbitrary"), vmem),
        name="in_proj_lru",
    )(h, w_stack, proj_a, proj_a, conv_w, vec(conv_b), w_ax, vec(b_a), vec(b_x), vec(lam))


def _mem_attn_kernel(q_ref, kv_ref, gate_ref, o_ref):
    scale = MEM_HEAD_DIM ** -0.5
    kv_width = MEM_HEADS * MEM_HEAD_DIM
    for hd in range(MEM_HEADS):
        cols = slice(hd * MEM_HEAD_DIM, (hd + 1) * MEM_HEAD_DIM)
        vcols = slice(kv_width + hd * MEM_HEAD_DIM, kv_width + (hd + 1) * MEM_HEAD_DIM)
        s = lax.dot_general(q_ref[:, cols], kv_ref[:, cols], (((1,), (1,)), ((), ())),
                            preferred_element_type=F32) * scale
        p = jnp.exp(s - jnp.max(s, axis=-1, keepdims=True))
        l = jnp.sum(p, axis=-1, keepdims=True)
        o = jnp.dot(p.astype(BF16), kv_ref[:, vcols], preferred_element_type=F32) / l
        o_ref[:, cols] = (o * _silu(gate_ref[:, cols].astype(F32))).astype(o_ref.dtype)


def _memory_attention(proj, mkv, *, seq, mem_len, col_q, col_gate, tq):
    n = proj.shape[0]
    width = MEM_HEADS * MEM_HEAD_DIM
    tiles_per_seq = seq // tq
    bq, bg = _col_block(col_q, width), _col_block(col_gate, width)
    vmem = 2 * (3 * tq * width * 2 + mem_len * 2 * width * 2) + 8 * tq * mem_len * 4 + (4 << 20)
    return pl.pallas_call(
        _mem_attn_kernel,
        grid=(n // tq,),
        in_specs=[
            pl.BlockSpec((tq, width), lambda i: (i, bq)),
            pl.BlockSpec((mem_len, 2 * width), lambda i: (i // tiles_per_seq, 0)),
            pl.BlockSpec((tq, width), lambda i: (i, bg)),
        ],
        out_specs=pl.BlockSpec((tq, width), lambda i: (i, 0)),
        out_shape=jax.ShapeDtypeStruct((n, width), BF16),
        compiler_params=_compiler_params(("arbitrary",), vmem),
        name="memory_attention",
    )(proj, mkv, proj)


def _merge_kernel(*refs):
    nb = (len(refs) - 1) // 3
    y_refs, w_refs, g_refs, o_ref = refs[:nb], refs[nb:2 * nb], refs[2 * nb:3 * nb], refs[-1]
    acc = None
    for y_ref, w_ref, g_ref in zip(y_refs, w_refs, g_refs):
        term = _sigmoid(g_ref[...].astype(F32)) * jnp.dot(y_ref[...], w_ref[...], preferred_element_type=F32)
        acc = term if acc is None else acc + term
    o_ref[...] = acc.astype(o_ref.dtype)


def _merge(branches, proj, w_branch, *, col_merge, tm, tn):
    n = proj.shape[0]
    d = w_branch.shape[1]
    sizes = [y.shape[1] for y in branches]
    offsets = [sum(sizes[:i]) for i in range(len(sizes))]
    y_specs = [pl.BlockSpec((tm, sz), lambda i, j: (i, 0)) for sz in sizes]
    w_specs = [pl.BlockSpec((sz, tn), lambda i, j, r=_col_block(off, sz): (r, j)) for sz, off in zip(sizes, offsets)]
    g_specs = [pl.BlockSpec((tm, tn), lambda i, j, c=_col_block(col_merge + bi * d, tn): (i, c + j))
               for bi in range(len(sizes))]
    total = sum(sizes)
    vmem = 2 * (tm * total * 2 + total * tn * 2 + len(sizes) * tm * tn * 2 + tm * tn * 2) + 4 * tm * tn * 4 + (4 << 20)
    return pl.pallas_call(
        _merge_kernel,
        grid=(n // tm, d // tn),
        in_specs=y_specs + w_specs + g_specs,
        out_specs=pl.BlockSpec((tm, tn), lambda i, j: (i, j)),
        out_shape=jax.ShapeDtypeStruct((n, d), BF16),
        compiler_params=_compiler_params(("arbitrary", "arbitrary"), vmem),
        name="branch_merge",
    )(*branches, *([w_branch] * len(sizes)), *([proj] * len(sizes)))


def _out_proj_kernel(x_ref, m_ref, w_ref, g_ref, *o_refs, last):
    y = x_ref[...] + jnp.dot(m_ref[...], w_ref[...], preferred_element_type=F32)
    normed = _rms(y, g_ref[...])
    if last:
        o_refs[0][...] = normed
    else:
        o_refs[0][...] = y
        o_refs[1][...] = normed.astype(BF16)


def _out_proj(x, merged, w_out, g, *, last, tm):
    n, d = x.shape
    vmem = 2 * (2 * tm * d * 4 + 2 * tm * d * 2 + d * d * 2) + 3 * tm * d * 4 + (4 << 20)
    row_spec = pl.BlockSpec((tm, d), lambda i: (i, 0))
    out_shape = [jax.ShapeDtypeStruct((n, d), F32)] + ([] if last else [jax.ShapeDtypeStruct((n, d), BF16)])
    return pl.pallas_call(
        functools.partial(_out_proj_kernel, last=last),
        grid=(n // tm,),
        in_specs=[row_spec, row_spec, pl.BlockSpec((d, d), lambda i: (0, 0)), pl.BlockSpec((1, d), lambda i: (0, 0))],
        out_specs=[row_spec] * len(out_shape),
        out_shape=out_shape,
        compiler_params=_compiler_params(("arbitrary",), vmem),
        name="out_proj",
    )(x, merged, w_out, g.reshape(1, d))


def kernel(x, mem, norm_g, w_in, lam_vecs, subln_g, pool_w, pool_scale, conv_w, conv_b, lru_wa, lru_ba,
           lru_wx, lru_bx, lru_lambda, mem_norm_g, w_mem_kv, w_branch, w_out, final_g):
    batch, seq, d = x.shape
    mem_len = mem.shape[1]
    depth = w_in.shape[0]
    da_width = DA_HEADS * DA_V_DIM
    pool_width = pool_scale.shape[-1]
    lru_width = conv_b.shape[-1]
    mem_width = MEM_HEADS * MEM_HEAD_DIM
    branch_width = da_width + pool_width + lru_width + mem_width
    col_q = 0
    col_k = col_q + da_width
    col_v = col_k + da_width
    col_pool = col_v + da_width
    col_lru = col_pool + pool_width
    col_qmem = col_lru + lru_width
    col_gate = col_qmem + mem_width
    col_merge = col_gate + branch_width
    gate_a = col_gate
    gate_b = gate_a + da_width
    gate_c = gate_b + pool_width
    gate_m = gate_c + lru_width

    xf = x.reshape(batch * seq, d)
    memf = mem.reshape(batch * mem_len, d)
    h = _norm(xf, norm_g[0], tm=512)
    for l in range(depth):
        last = l == depth - 1
        lambda_init = 0.8 - 0.6 * math.exp(-0.3 * l)
        proj = _proj(h, w_in, l, panels=_col_block(col_merge, PROJ_TN), tm=2048, tn=PROJ_TN, name="in_proj")
        w_ax = jnp.concatenate([lru_wa[l], lru_wx[l]], axis=-1).astype(BF16)
        merge_gates, y_c = _proj_lru(h, w_in, l, proj, conv_w[l], conv_b[l], w_ax, lru_ba[l], lru_bx[l],
                                     lru_lambda[l], seq=seq, panel0=_col_block(col_merge, PROJ_TN),
                                     col_x=col_lru, col_gate=gate_c, tm=1024, tn=PROJ_TN)
        mkv = _norm_proj(memf, mem_norm_g[l], w_mem_kv[l].astype(BF16), tm=batch * mem_len, tn=512, name="mem_kv")
        y_a = _diff_attention(proj, lam_vecs[l], subln_g[l], batch=batch, seq=seq, col_q=col_q, col_k=col_k,
                              col_v=col_v, col_gate=gate_a, lambda_init=lambda_init, tq=512)
        y_b = _multiscale_pool(proj, pool_w[l].astype(BF16), pool_scale[l], seq=seq, col_x=col_pool,
                               col_gate=gate_b, ts=1024)
        y_m = _memory_attention(proj, mkv, seq=seq, mem_len=mem_len, col_q=col_qmem, col_gate=gate_m, tq=1024)
        merged = _merge([y_a, y_b, y_c, y_m], merge_gates, w_branch[l].astype(BF16), col_merge=0, tm=1024, tn=512)
        if last:
            (out,) = _out_proj(xf, merged, w_out[l].astype(BF16), final_g, last=True, tm=512)
        else:
            xf, h = _out_proj(xf, merged, w_out[l].astype(BF16), norm_g[l + 1], last=False, tm=512)
    return out.reshape(batch, seq, d)
```

```python
import functools
import math

import jax
import jax.numpy as jnp
from jax import lax
from jax.experimental import pallas as pl
from jax.experimental.pallas import tpu as pltpu

F32 = jnp.float32
BF16 = jnp.bfloat16

EPS = 1e-6
DA_HEADS = 8
DA_HEAD_DIM = 64
DA_V_DIM = 2 * DA_HEAD_DIM
POOL_WINDOWS = (2, 4, 8, 16)
POOL_HALO = 32
LRU_BLOCKS = 8
CONV_WIDTH = 4
CONV_HALO = 8
LRU_C = 8.0
MEM_HEADS = 4
MEM_HEAD_DIM = 128
LANES = 128
SUBLANES = 8
V7X_VMEM_BYTES = 64 * 1024 * 1024
PROJ_TN = 1024
MASK_VALUE = -1e30
LOG2E = 1.4426950408889634


def _compiler_params(semantics, vmem_bytes):
    assert vmem_bytes < V7X_VMEM_BYTES
    return pltpu.CompilerParams(dimension_semantics=semantics, vmem_limit_bytes=int(vmem_bytes))


def _col_block(col, width):
    assert col % width == 0, (col, width)
    return col // width


def _rms(x, g):
    ms = jnp.mean(x * x, axis=-1, keepdims=True)
    return x * lax.rsqrt(ms + EPS) * g


def _segment_pitch(seg):
    return seg + (4 - seg) % 8


def _lane_tile(x, n):
    return jnp.concatenate([x] * n, axis=1)


def _sigmoid(x):
    return 0.5 * jnp.tanh(0.5 * x) + 0.5


def _silu(x):
    return x * _sigmoid(x)


def _norm_proj_kernel(x_ref, g_ref, w_ref, o_ref, h_ref):
    @pl.when(pl.program_id(1) == 0)
    def _():
        h_ref[...] = _rms(x_ref[...], g_ref[...]).astype(BF16)

    o_ref[...] = jnp.dot(h_ref[...], w_ref[...], preferred_element_type=F32).astype(o_ref.dtype)


def _norm_proj(x, g, w, *, tm, tn, name):
    n, d = x.shape
    nc = w.shape[1]
    vmem = 2 * (tm * d * 4 + d * tn * 2 + tm * tn * 2) + tm * d * 2 + (4 << 20)
    return pl.pallas_call(
        _norm_proj_kernel,
        grid=(n // tm, nc // tn),
        in_specs=[
            pl.BlockSpec((tm, d), lambda i, j: (i, 0)),
            pl.BlockSpec((1, d), lambda i, j: (0, 0)),
            pl.BlockSpec((d, tn), lambda i, j: (0, j)),
        ],
        out_specs=pl.BlockSpec((tm, tn), lambda i, j: (i, j)),
        out_shape=jax.ShapeDtypeStruct((n, nc), BF16),
        scratch_shapes=[pltpu.VMEM((tm, d), BF16)],
        compiler_params=_compiler_params(("arbitrary", "arbitrary"), vmem),
        name=name,
    )(x, g.reshape(1, d), w)


def _norm_kernel(x_ref, g_ref, o_ref):
    o_ref[...] = _rms(x_ref[...], g_ref[...]).astype(o_ref.dtype)


def _norm(x, g, *, tm):
    n, d = x.shape
    vmem = 2 * (tm * d * 4 + tm * d * 2) + 3 * tm * d * 4 + (4 << 20)
    return pl.pallas_call(
        _norm_kernel,
        grid=(n // tm,),
        in_specs=[pl.BlockSpec((tm, d), lambda i: (i, 0)), pl.BlockSpec((1, d), lambda i: (0, 0))],
        out_specs=pl.BlockSpec((tm, d), lambda i: (i, 0)),
        out_shape=jax.ShapeDtypeStruct((n, d), BF16),
        compiler_params=_compiler_params(("arbitrary",), vmem),
        name="rms_norm",
    )(x, g.reshape(1, d))


def _proj_kernel(h_ref, w_ref, o_ref, wb_ref):
    @pl.when(pl.program_id(1) == 0)
    def _():
        wb_ref[...] = w_ref[...].astype(BF16)

    o_ref[...] = jnp.dot(h_ref[...], wb_ref[...], preferred_element_type=F32).astype(o_ref.dtype)


def _proj(h, w_stack, layer, *, panels, tm, tn, name):
    n, d = h.shape
    vmem = 2 * (tm * d * 2 + d * tn * 4 + tm * tn * 2) + d * tn * 2 + tm * tn * 4 + (4 << 20)
    return pl.pallas_call(
        _proj_kernel,
        grid=(panels, n // tm),
        in_specs=[
            pl.BlockSpec((tm, d), lambda j, i: (i, 0)),
            pl.BlockSpec((None, d, tn), lambda j, i: (layer, 0, j)),
        ],
        out_specs=pl.BlockSpec((tm, tn), lambda j, i: (i, j)),
        out_shape=jax.ShapeDtypeStruct((n, panels * tn), BF16),
        scratch_shapes=[pltpu.VMEM((d, tn), BF16)],
        compiler_params=_compiler_params(("arbitrary", "arbitrary"), vmem),
        name=name,
    )(h, w_stack)


def _diff_attn_kernel(q_ref, k_ref, v_ref, gate_ref, lamv_ref, subg_ref, o_ref,
                      qs_ref, vx_ref, m_ref, acc_ref, sa_ref, sb_ref, *, tq, lambda_init):
    g = pl.program_id(2)
    rows = 2 * tq
    hw = DA_V_DIM
    scale2 = (DA_HEAD_DIM ** -0.5) * LOG2E

    @pl.when(g == 0)
    def _():
        vx_ref[:, 0:hw] = v_ref[...]
        vx_ref[:, hw:2 * hw] = jnp.ones(v_ref.shape, v_ref.dtype)

    for t in range(2):
        q = q_ref[t * tq:(t + 1) * tq, :]
        lane = lax.broadcasted_iota(jnp.int32, q.shape, 1)
        zero = jnp.zeros_like(q)
        qs_ref[t * rows:t * rows + tq, :] = jnp.where(lane < DA_HEAD_DIM, q, zero)
        qs_ref[t * rows + tq:(t + 1) * rows, :] = jnp.where(lane >= DA_HEAD_DIM, q, zero)
    m_ref[...] = jnp.full(m_ref.shape, MASK_VALUE, F32)
    acc_ref[...] = jnp.zeros(acc_ref.shape, F32)

    def scores(j, s_ref, row0, nrows):
        k = k_ref[pl.ds(pl.multiple_of(j * tq, tq), tq), :]
        s_ref[0:nrows, :] = lax.dot_general(qs_ref[row0:row0 + nrows, :], k, (((1,), (1,)), ((), ())),
                                            preferred_element_type=F32)

    def update(j, s_ref, row0, nrows, diag_tile):
        s = s_ref[0:nrows, :]
        if diag_tile is not None:
            assert row0 == diag_tile * rows
            causal = (lax.broadcasted_iota(jnp.int32, (tq, tq), 1)
                      <= lax.broadcasted_iota(jnp.int32, (tq, tq), 0))
            pieces = [jnp.where(causal, s[k * tq:(k + 1) * tq, :], MASK_VALUE) for k in range(2)]
            s = jnp.concatenate(pieces + ([s[rows:nrows, :]] if nrows > rows else []), axis=0)
        m_prev = m_ref[row0:row0 + nrows, :]
        m_next = jnp.maximum(m_prev, jnp.max(s, axis=-1, keepdims=True))
        p = jnp.exp2((s - _lane_tile(m_next, tq // LANES)) * scale2)
        alpha = jnp.exp2((m_prev - m_next) * scale2)
        vx = vx_ref[pl.ds(pl.multiple_of(j * tq, tq), tq), :]
        pv = jnp.dot(p.astype(BF16), vx, preferred_element_type=F32)
        acc_ref[row0:row0 + nrows, :] = _lane_tile(alpha, 2) * acc_ref[row0:row0 + nrows, :] + pv
        m_ref[row0:row0 + nrows, :] = m_next

    lv = lamv_ref[...]
    lam = (jnp.exp(jnp.sum(lv[0:1, :] * lv[1:2, :], axis=-1, keepdims=True))
           - jnp.exp(jnp.sum(lv[2:3, :] * lv[3:4, :], axis=-1, keepdims=True)) + lambda_init)

    def finalize(t):
        o_maps = acc_ref[t * rows:(t + 1) * rows, 0:hw] / acc_ref[t * rows:(t + 1) * rows, hw:2 * hw]
        o = o_maps[0:tq, :] - lam * o_maps[tq:rows, :]
        y = _rms(o, subg_ref[...]) * (1.0 - lambda_init)
        out_rows = slice(t * tq, (t + 1) * tq)
        o_ref[out_rows, :] = (y * _silu(gate_ref[out_rows, :].astype(F32))).astype(o_ref.dtype)

    scores(0, sa_ref, 0, 2 * rows)

    def unmasked_pair(jj, carry):
        scores(2 * jj + 1, sb_ref, 0, 2 * rows)
        update(2 * jj, sa_ref, 0, 2 * rows, None)
        scores(2 * jj + 2, sa_ref, 0, 2 * rows)
        update(2 * jj + 1, sb_ref, 0, 2 * rows, None)
        return carry

    lax.fori_loop(0, g, unmasked_pair, 0)
    scores(2 * g + 1, sb_ref, rows, rows)
    update(2 * g, sa_ref, 0, 2 * rows, 0)
    finalize(0)
    update(2 * g + 1, sb_ref, rows, rows, 1)
    finalize(1)


def _diff_attention(proj, lam_vecs, subln_g, *, batch, seq, col_q, col_k, col_v, col_gate, lambda_init, tq):
    n = batch * seq
    assert seq % (2 * tq) == 0
    ng = seq // (2 * tq)
    hw = DA_V_DIM
    bq, bk, bv, bg = (_col_block(c, hw) for c in (col_q, col_k, col_v, col_gate))
    vmem = (2 * (3 * 2 * tq * hw * 2 + 2 * seq * hw * 2) + 2 * 2 * tq * hw * (2 + 12) + seq * 2 * hw * 2
            + 6 * 4 * tq * tq * 4 + (4 << 20))
    kernel = functools.partial(_diff_attn_kernel, tq=tq, lambda_init=lambda_init)
    return pl.pallas_call(
        kernel,
        grid=(batch, DA_HEADS, ng),
        in_specs=[
            pl.BlockSpec((2 * tq, hw), lambda b, h, i: (b * ng + i, bq + h)),
            pl.BlockSpec((seq, hw), lambda b, h, i: (b, bk + h)),
            pl.BlockSpec((seq, hw), lambda b, h, i: (b, bv + h)),
            pl.BlockSpec((2 * tq, hw), lambda b, h, i: (b * ng + i, bg + h)),
            pl.BlockSpec(lam_vecs.shape, lambda b, h, i: (0, 0)),
            pl.BlockSpec((1, hw), lambda b, h, i: (0, 0)),
        ],
        out_specs=pl.BlockSpec((2 * tq, hw), lambda b, h, i: (b * ng + i, h)),
        out_shape=jax.ShapeDtypeStruct((n, DA_HEADS * hw), BF16),
        scratch_shapes=[
            pltpu.VMEM((4 * tq, hw), BF16),
            pltpu.VMEM((seq, 2 * hw), BF16),
            pltpu.VMEM((4 * tq, LANES), F32),
            pltpu.VMEM((4 * tq, 2 * hw), F32),
            pltpu.VMEM((4 * tq, tq), F32),
            pltpu.VMEM((4 * tq, tq), F32),
        ],
        compiler_params=_compiler_params(("arbitrary", "arbitrary", "arbitrary"), vmem),
        name="diff_attention",
    )(proj, proj, proj, proj, lam_vecs, subln_g.reshape(1, hw))


def _gate_cols(gate_refs, cols):
    piece = gate_refs[0].shape[1]
    idx, start = divmod(cols.start, piece)
    assert cols.stop - cols.start <= piece - start
    return _silu(gate_refs[idx][:, start:start + cols.stop - cols.start].astype(F32))


def _pool_kernel(x_ref, halo_ref, gate_lo_ref, gate_hi_ref, w_ref, scale_ref, o_ref, xe_ref, la_ref, lb_ref,
                 *, ts, tiles_per_seq):
    gate_refs = (gate_lo_ref, gate_hi_ref)
    t = pl.program_id(0) % tiles_per_seq
    halo = halo_ref[...].astype(F32)
    xe_ref[0:POOL_HALO, :] = jnp.where(t == 0, jnp.zeros_like(halo), halo)
    x = x_ref[...].astype(F32)
    xe_ref[POOL_HALO:POOL_HALO + ts, :] = x
    gw = x.shape[1] // len(POOL_WINDOWS)
    pos1 = t * ts + lax.broadcasted_iota(jnp.int32, (ts, gw), 0) + 1
    for gi, w in enumerate(POOL_WINDOWS):
        cols = slice(gi * gw, (gi + 1) * gw)
        xg = x[:, cols]
        levels = w.bit_length() - 1
        src_ref, src_cols, spare = xe_ref, cols, [la_ref, lb_ref]
        for lv in range(levels):
            shift = 1 << lv
            lo = POOL_HALO if lv == levels - 1 else SUBLANES * (lv + 1)
            nrows = POOL_HALO + ts - lo
            wsum = src_ref[lo:lo + nrows, src_cols] + src_ref[lo - shift:lo - shift + nrows, src_cols]
            if lv < levels - 1:
                dst_ref = spare[lv % 2]
                dst_ref[lo:lo + nrows, :] = wsum
                src_ref, src_cols = dst_ref, slice(0, gw)
        count = jnp.minimum(pos1, w).astype(F32)
        d = (wsum / count - xg).astype(BF16)
        y = jnp.dot(d, w_ref[gi], preferred_element_type=F32) * scale_ref[:, cols]
        o_ref[:, cols] = (y * _gate_cols(gate_refs, cols)).astype(o_ref.dtype)


def _multiscale_pool(proj, pool_w, pool_scale, *, seq, col_x, col_gate, ts):
    n = proj.shape[0]
    width = pool_scale.shape[-1]
    tiles_per_seq = seq // ts
    halo_per_tile = ts // POOL_HALO
    half = width // 2
    bx, bg = _col_block(col_x, width), _col_block(col_gate, half)
    vmem = 2 * (3 * ts * width * 2 + POOL_HALO * width * 2 + pool_w.size * 2) + 8 * ts * width * 4 + (4 << 20)
    kernel = functools.partial(_pool_kernel, ts=ts, tiles_per_seq=tiles_per_seq)
    return pl.pallas_call(
        kernel,
        grid=(n // ts,),
        in_specs=[
            pl.BlockSpec((ts, width), lambda i: (i, bx)),
            pl.BlockSpec((POOL_HALO, width), lambda i: (jnp.maximum(i * halo_per_tile - 1, 0), bx)),
            pl.BlockSpec((ts, half), lambda i: (i, bg)),
            pl.BlockSpec((ts, half), lambda i: (i, bg + 1)),
            pl.BlockSpec(pool_w.shape, lambda i: (0, 0, 0)),
            pl.BlockSpec((1, width), lambda i: (0, 0)),
        ],
        out_specs=pl.BlockSpec((ts, width), lambda i: (i, 0)),
        out_shape=jax.ShapeDtypeStruct((n, width), BF16),
        scratch_shapes=[pltpu.VMEM((POOL_HALO + ts, width), F32)]
        + [pltpu.VMEM((POOL_HALO + ts, width // len(POOL_WINDOWS)), F32)] * 2,
        compiler_params=_compiler_params(("arbitrary",), vmem),
        name="multiscale_pool",
    )(proj, proj, proj, proj, pool_w, pool_scale.reshape(1, width))


def _proj_lru_kernel(h_ref, w_ref, x_ref, gate_ref, cw_ref, cb_ref, wax_ref, ba_ref, bx_ref, lam_ref,
                     o_ref, y_ref, wb_ref, xe_ref, carry_ref, a_ref, u_ref, *, tiles_per_seq):
    i = pl.program_id(1)
    t = i % tiles_per_seq
    ts, bw = x_ref.shape

    @pl.when(i == 0)
    def _():
        wb_ref[...] = w_ref[...].astype(BF16)

    @pl.when(t == 0)
    def _():
        xe_ref[0:CONV_HALO, :] = jnp.zeros((CONV_HALO, bw), F32)
        carry_ref[...] = jnp.zeros(carry_ref.shape, F32)

    @pl.when(t > 0)
    def _():
        xe_ref[0:CONV_HALO, :] = xe_ref[ts:ts + CONV_HALO, :]

    xe_ref[CONV_HALO:CONV_HALO + ts, :] = x_ref[...].astype(F32)

    xc = cb_ref[...] + cw_ref[CONV_WIDTH - 1:CONV_WIDTH, :] * xe_ref[CONV_HALO:CONV_HALO + ts, :]
    for back in range(1, CONV_WIDTH):
        tap = CONV_WIDTH - 1 - back
        xc = xc + cw_ref[tap:tap + 1, :] * xe_ref[CONV_HALO - back:CONV_HALO - back + ts, :]

    neg_lam = -lam_ref[...]
    softplus = jnp.maximum(neg_lam, 0.0) + jnp.log(1.0 + jnp.exp(-jnp.abs(neg_lam)))
    z = jnp.dot(xc.astype(BF16), wax_ref[...], preferred_element_type=F32)
    o_ref[...] = jnp.dot(h_ref[...], wb_ref[...], preferred_element_type=F32).astype(o_ref.dtype)

    r = _sigmoid(z[:, 0:bw] + ba_ref[...])
    gate_i = _sigmoid(z[:, bw:2 * bw] + bx_ref[...])
    log_a = (-LRU_C) * r * softplus
    th = jnp.tanh(log_a)
    m2 = -2.0 * th / (1.0 - th)
    mult = jnp.where(m2 > 0.0, m2 * lax.rsqrt(m2), 0.0)
    row = lax.broadcasted_iota(jnp.int32, (ts, bw), 0)
    first_row = jnp.where(t == 0, 0, -1)
    mult = jnp.where(row == first_row, 1.0, mult)
    a_all = jnp.exp(log_a)
    u_all = mult * (gate_i * xc)

    seg = ts // SUBLANES
    pitch = _segment_pitch(seg)
    for i in range(SUBLANES):
        a_ref[i * pitch:i * pitch + seg, :] = a_all[i * seg:(i + 1) * seg, :]
        u_ref[i * pitch:i * pitch + seg, :] = u_all[i * seg:(i + 1) * seg, :]
    step = lambda ref, j: ref[pl.ds(j, SUBLANES, stride=pitch), :]
    h_end = jnp.zeros((SUBLANES, bw), F32)
    a_end = jnp.ones((SUBLANES, bw), F32)
    for j in range(seg):
        a_j = step(a_ref, j)
        h_end = a_j * h_end + step(u_ref, j)
        a_end = a_j * a_end
    sub = lax.broadcasted_iota(jnp.int32, (SUBLANES, bw), 0)
    for k in (1, 2, 4):
        keep = sub >= k
        a_prev = jnp.where(keep, pltpu.roll(a_end, k, 0), 1.0)
        h_prev = jnp.where(keep, pltpu.roll(h_end, k, 0), 0.0)
        h_end = a_end * h_prev + h_end
        a_end = a_end * a_prev
    carry_in = carry_ref[...]
    state = a_end * carry_in + h_end
    carry_ref[...] = jnp.broadcast_to(state[SUBLANES - 1:SUBLANES, :], state.shape)
    h = jnp.where(sub >= 1, pltpu.roll(state, 1, 0), carry_in)
    for j in range(seg):
        h = step(a_ref, j) * h + step(u_ref, j)
        u_ref[pl.ds(j, SUBLANES, stride=pitch), :] = h
    hs = jnp.concatenate([u_ref[i * pitch:i * pitch + seg, :] for i in range(SUBLANES)], axis=0)
    y_ref[...] = (hs * _silu(gate_ref[...].astype(F32))).astype(y_ref.dtype)


def _proj_lru(h, w_stack, layer, proj_a, conv_w, conv_b, w_ax, b_a, b_x, lam, *, seq, panel0, col_x, col_gate,
              tm, tn):
    n, d = h.shape
    assert w_stack.shape[2] == (panel0 + LRU_BLOCKS) * tn and seq % tm == 0
    width = conv_b.shape[-1]
    bw = width // LRU_BLOCKS
    bx, bg = _col_block(col_x, bw), _col_block(col_gate, bw)
    vec = lambda a: a.reshape(1, width)
    vmem = (2 * (tm * d * 2 + d * tn * 4 + tm * tn * 2) + d * tn * 2 + tm * tn * 4
            + 24 * tm * bw * 4 + (4 << 20))
    kernel = functools.partial(_proj_lru_kernel, tiles_per_seq=seq // tm)
    vec_spec = pl.BlockSpec((1, bw), lambda p, i: (0, p))
    return pl.pallas_call(
        kernel,
        grid=(LRU_BLOCKS, n // tm),
        in_specs=[
            pl.BlockSpec((tm, d), lambda p, i: (i, 0)),
            pl.BlockSpec((None, d, tn), lambda p, i: (layer, 0, panel0 + p)),
            pl.BlockSpec((tm, bw), lambda p, i: (i, bx + p)),
            pl.BlockSpec((tm, bw), lambda p, i: (i, bg + p)),
            pl.BlockSpec((CONV_WIDTH, bw), lambda p, i: (0, p)),
            vec_spec,
            pl.BlockSpec((None, bw, 2 * bw), lambda p, i: (p, 0, 0)),
            vec_spec, vec_spec, vec_spec,
        ],
        out_specs=[
            pl.BlockSpec((tm, tn), lambda p, i: (i, p)),
            pl.BlockSpec((tm, bw), lambda p, i: (i, p)),
        ],
        out_shape=[jax.ShapeDtypeStruct((n, LRU_BLOCKS * tn), BF16), jax.ShapeDtypeStruct((n, width), BF16)],
        scratch_shapes=[
            pltpu.VMEM((d, tn), BF16),
            pltpu.VMEM((CONV_HALO + tm, bw), F32),
            pltpu.VMEM((SUBLANES, bw), F32),
            pltpu.VMEM((SUBLANES * _segment_pitch(tm // SUBLANES), bw), F32),
            pltpu.VMEM((SUBLANES * _segment_pitch(tm // SUBLANES), bw), F32),
        ],
        compiler_params=_compiler_params(("arbitrary", "arbitrary"), vmem),
        name="in_proj_lru",
    )(h, w_stack, proj_a, proj_a, conv_w, vec(conv_b), w_ax, vec(b_a), vec(b_x), vec(lam))


def _mem_attn_kernel(q_ref, kv_ref, gate_ref, o_ref):
    scale = MEM_HEAD_DIM ** -0.5
    kv_width = MEM_HEADS * MEM_HEAD_DIM
    for hd in range(MEM_HEADS):
        cols = slice(hd * MEM_HEAD_DIM, (hd + 1) * MEM_HEAD_DIM)
        vcols = slice(kv_width + hd * MEM_HEAD_DIM, kv_width + (hd + 1) * MEM_HEAD_DIM)
        s = lax.dot_general(q_ref[:, cols], kv_ref[:, cols], (((1,), (1,)), ((), ())),
                            preferred_element_type=F32) * scale
        p = jnp.exp(s - jnp.max(s, axis=-1, keepdims=True))
        l = jnp.sum(p, axis=-1, keepdims=True)
        o = jnp.dot(p.astype(BF16), kv_ref[:, vcols], preferred_element_type=F32) / l
        o_ref[:, cols] = (o * _silu(gate_ref[:, cols].astype(F32))).astype(o_ref.dtype)


def _memory_attention(proj, mkv, *, seq, mem_len, col_q, col_gate, tq):
    n = proj.shape[0]
    width = MEM_HEADS * MEM_HEAD_DIM
    tiles_per_seq = seq // tq
    bq, bg = _col_block(col_q, width), _col_block(col_gate, width)
    vmem = 2 * (3 * tq * width * 2 + mem_len * 2 * width * 2) + 8 * tq * mem_len * 4 + (4 << 20)
    return pl.pallas_call(
        _mem_attn_kernel,
        grid=(n // tq,),
        in_specs=[
            pl.BlockSpec((tq, width), lambda i: (i, bq)),
            pl.BlockSpec((mem_len, 2 * width), lambda i: (i // tiles_per_seq, 0)),
            pl.BlockSpec((tq, width), lambda i: (i, bg)),
        ],
        out_specs=pl.BlockSpec((tq, width), lambda i: (i, 0)),
        out_shape=jax.ShapeDtypeStruct((n, width), BF16),
        compiler_params=_compiler_params(("arbitrary",), vmem),
        name="memory_attention",
    )(proj, mkv, proj)


def _merge_kernel(*refs):
    nb = (len(refs) - 1) // 3
    y_refs, w_refs, g_refs, o_ref = refs[:nb], refs[nb:2 * nb], refs[2 * nb:3 * nb], refs[-1]
    acc = None
    for y_ref, w_ref, g_ref in zip(y_refs, w_refs, g_refs):
        term = _sigmoid(g_ref[...].astype(F32)) * jnp.dot(y_ref[...], w_ref[...], preferred_element_type=F32)
        acc = term if acc is None else acc + term
    o_ref[...] = acc.astype(o_ref.dtype)


def _merge(branches, proj, w_branch, *, col_merge, tm, tn):
    n = proj.shape[0]
    d = w_branch.shape[1]
    sizes = [y.shape[1] for y in branches]
    offsets = [sum(sizes[:i]) for i in range(len(sizes))]
    y_specs = [pl.BlockSpec((tm, sz), lambda i, j: (i, 0)) for sz in sizes]
    w_specs = [pl.BlockSpec((sz, tn), lambda i, j, r=_col_block(off, sz): (r, j)) for sz, off in zip(sizes, offsets)]
    g_specs = [pl.BlockSpec((tm, tn), lambda i, j, c=_col_block(col_merge + bi * d, tn): (i, c + j))
               for bi in range(len(sizes))]
    total = sum(sizes)
    vmem = 2 * (tm * total * 2 + total * tn * 2 + len(sizes) * tm * tn * 2 + tm * tn * 2) + 4 * tm * tn * 4 + (4 << 20)
    return pl.pallas_call(
        _merge_kernel,
        grid=(n // tm, d // tn),
        in_specs=y_specs + w_specs + g_specs,
        out_specs=pl.BlockSpec((tm, tn), lambda i, j: (i, j)),
        out_shape=jax.ShapeDtypeStruct((n, d), BF16),
        compiler_params=_compiler_params(("arbitrary", "arbitrary"), vmem),
        name="branch_merge",
    )(*branches, *([w_branch] * len(sizes)), *([proj] * len(sizes)))


def _out_proj_kernel(x_ref, m_ref, w_ref, g_ref, *o_refs, last):
    y = x_ref[...] + jnp.dot(m_ref[...], w_ref[...], preferred_element_type=F32)
    normed = _rms(y, g_ref[...])
    if last:
        o_refs[0][...] = normed
    else:
        o_refs[0][...] = y
        o_refs[1][...] = normed.astype(BF16)


def _out_proj(x, merged, w_out, g, *, last, tm):
    n, d = x.shape
    vmem = 2 * (2 * tm * d * 4 + 2 * tm * d * 2 + d * d * 2) + 3 * tm * d * 4 + (4 << 20)
    row_spec = pl.BlockSpec((tm, d), lambda i: (i, 0))
    out_shape = [jax.ShapeDtypeStruct((n, d), F32)] + ([] if last else [jax.ShapeDtypeStruct((n, d), BF16)])
    return pl.pallas_call(
        functools.partial(_out_proj_kernel, last=last),
        grid=(n // tm,),
        in_specs=[row_spec, row_spec, pl.BlockSpec((d, d), lambda i: (0, 0)), pl.BlockSpec((1, d), lambda i: (0, 0))],
        out_specs=[row_spec] * len(out_shape),
        out_shape=out_shape,
        compiler_params=_compiler_params(("arbitrary",), vmem),
        name="out_proj",
    )(x, merged, w_out, g.reshape(1, d))


def kernel(x, mem, norm_g, w_in, lam_vecs, subln_g, pool_w, pool_scale, conv_w, conv_b, lru_wa, lru_ba,
           lru_wx, lru_bx, lru_lambda, mem_norm_g, w_mem_kv, w_branch, w_out, final_g):
    batch, seq, d = x.shape
    mem_len = mem.shape[1]
    depth = w_in.shape[0]
    da_width = DA_HEADS * DA_V_DIM
    pool_width = pool_scale.shape[-1]
    lru_width = conv_b.shape[-1]
    mem_width = MEM_HEADS * MEM_HEAD_DIM
    branch_width = da_width + pool_width + lru_width + mem_width
    col_q = 0
    col_k = col_q + da_width
    col_v = col_k + da_width
    col_pool = col_v + da_width
    col_lru = col_pool + pool_width
    col_qmem = col_lru + lru_width
    col_gate = col_qmem + mem_width
    col_merge = col_gate + branch_width
    gate_a = col_gate
    gate_b = gate_a + da_width
    gate_c = gate_b + pool_width
    gate_m = gate_c + lru_width

    xf = x.reshape(batch * seq, d)
    memf = mem.reshape(batch * mem_len, d)
    h = _norm(xf, norm_g[0], tm=512)
    for l in range(depth):
        last = l == depth - 1
        lambda_init = 0.8 - 0.6 * math.exp(-0.3 * l)
        proj = _proj(h, w_in, l, panels=_col_block(col_merge, PROJ_TN), tm=2048, tn=PROJ_TN, name="in_proj")
        w_ax = jnp.concatenate([lru_wa[l], lru_wx[l]], axis=-1).astype(BF16)
        merge_gates, y_c = _proj_lru(h, w_in, l, proj, conv_w[l], conv_b[l], w_ax, lru_ba[l], lru_bx[l],
                                     lru_lambda[l], seq=seq, panel0=_col_block(col_merge, PROJ_TN),
                                     col_x=col_lru, col_gate=gate_c, tm=1024, tn=PROJ_TN)
        mkv = _norm_proj(memf, mem_norm_g[l], w_mem_kv[l].astype(BF16), tm=batch * mem_len, tn=512, name="mem_kv")
        y_a = _diff_attention(proj, lam_vecs[l], subln_g[l], batch=batch, seq=seq, col_q=col_q, col_k=col_k,
                              col_v=col_v, col_gate=gate_a, lambda_init=lambda_init, tq=512)
        y_b = _multiscale_pool(proj, pool_w[l].astype(BF16), pool_scale[l], seq=seq, col_x=col_pool,
                               col_gate=gate_b, ts=1024)
        y_m = _memory_attention(proj, mkv, seq=seq, mem_len=mem_len, col_q=col_qmem, col_gate=gate_m, tq=1024)
        merged = _merge([y_a, y_b, y_c, y_m], merge_gates, w_branch[l].astype(BF16), col_merge=0, tm=1024, tn=512)
        if last:
            (out,) = _out_proj(xf, merged, w_out[l].astype(BF16), final_g, last=True, tm=512)
        else:
            xf, h = _out_proj(xf, merged, w_out[l].astype(BF16), norm_g[l + 1], last=False, tm=512)
    return out.reshape(batch, seq, d)
```

```python
import functools
import math

import jax
import jax.numpy as jnp
from jax import lax
from jax.experimental import pallas as pl
from jax.experimental.pallas import tpu as pltpu

F32 = jnp.float32
BF16 = jnp.bfloat16

EPS = 1e-6
DA_HEADS = 8
DA_HEAD_DIM = 64
DA_V_DIM = 2 * DA_HEAD_DIM
POOL_WINDOWS = (2, 4, 8, 16)
POOL_HALO = 32
LRU_BLOCKS = 8
CONV_WIDTH = 4
CONV_HALO = 8
LRU_C = 8.0
MEM_HEADS = 4
MEM_HEAD_DIM = 128
LANES = 128
SUBLANES = 8
V7X_VMEM_BYTES = 64 * 1024 * 1024
PROJ_TN = 1024
MASK_VALUE = -1e30
LOG2E = 1.4426950408889634


def _compiler_params(semantics, vmem_bytes):
    assert vmem_bytes < V7X_VMEM_BYTES
    return pltpu.CompilerParams(dimension_semantics=semantics, vmem_limit_bytes=int(vmem_bytes))


def _col_block(col, width):
    assert col % width == 0, (col, width)
    return col // width


def _rms(x, g):
    ms = jnp.mean(x * x, axis=-1, keepdims=True)
    return x * lax.rsqrt(ms + EPS) * g


def _segment_pitch(seg):
    return seg + (4 - seg) % 8


def _lane_tile(x, n):
    return jnp.concatenate([x] * n, axis=1)


def _sigmoid(x):
    return 0.5 * jnp.tanh(0.5 * x) + 0.5


def _silu(x):
    return x * _sigmoid(x)


def _norm_proj_kernel(x_ref, g_ref, w_ref, o_ref, h_ref):
    @pl.when(pl.program_id(1) == 0)
    def _():
        h_ref[...] = _rms(x_ref[...], g_ref[...]).astype(BF16)

    o_ref[...] = jnp.dot(h_ref[...], w_ref[...], preferred_element_type=F32).astype(o_ref.dtype)


def _norm_proj(x, g, w, *, tm, tn, name):
    n, d = x.shape
    nc = w.shape[1]
    vmem = 2 * (tm * d * 4 + d * tn * 2 + tm * tn * 2) + tm * d * 2 + (4 << 20)
    return pl.pallas_call(
        _norm_proj_kernel,
        grid=(n // tm, nc // tn),
        in_specs=[
            pl.BlockSpec((tm, d), lambda i, j: (i, 0)),
            pl.BlockSpec((1, d), lambda i, j: (0, 0)),
            pl.BlockSpec((d, tn), lambda i, j: (0, j)),
        ],
        out_specs=pl.BlockSpec((tm, tn), lambda i, j: (i, j)),
        out_shape=jax.ShapeDtypeStruct((n, nc), BF16),
        scratch_shapes=[pltpu.VMEM((tm, d), BF16)],
        compiler_params=_compiler_params(("arbitrary", "arbitrary"), vmem),
        name=name,
    )(x, g.reshape(1, d), w)


def _norm_kernel(x_ref, g_ref, o_ref):
    o_ref[...] = _rms(x_ref[...], g_ref[...]).astype(o_ref.dtype)


def _norm(x, g, *, tm):
    n, d = x.shape
    vmem = 2 * (tm * d * 4 + tm * d * 2) + 3 * tm * d * 4 + (4 << 20)
    return pl.pallas_call(
        _norm_kernel,
        grid=(n // tm,),
        in_specs=[pl.BlockSpec((tm, d), lambda i: (i, 0)), pl.BlockSpec((1, d), lambda i: (0, 0))],
        out_specs=pl.BlockSpec((tm, d), lambda i: (i, 0)),
        out_shape=jax.ShapeDtypeStruct((n, d), BF16),
        compiler_params=_compiler_params(("arbitrary",), vmem),
        name="rms_norm",
    )(x, g.reshape(1, d))


def _proj_kernel(h_ref, w_ref, o_ref, wb_ref):
    @pl.when(pl.program_id(1) == 0)
    def _():
        wb_ref[...] = w_ref[...].astype(BF16)

    o_ref[...] = jnp.dot(h_ref[...], wb_ref[...], preferred_element_type=F32).astype(o_ref.dtype)


def _proj(h, w_stack, layer, *, panels, tm, tn, name):
    n, d = h.shape
    vmem = 2 * (tm * d * 2 + d * tn * 4 + tm * tn * 2) + d * tn * 2 + tm * tn * 4 + (4 << 20)
    return pl.pallas_call(
        _proj_kernel,
        grid=(panels, n // tm),
        in_specs=[
            pl.BlockSpec((tm, d), lambda j, i: (i, 0)),
            pl.BlockSpec((None, d, tn), lambda j, i: (layer, 0, j)),
        ],
        out_specs=pl.BlockSpec((tm, tn), lambda j, i: (i, j)),
        out_shape=jax.ShapeDtypeStruct((n, panels * tn), BF16),
        scratch_shapes=[pltpu.VMEM((d, tn), BF16)],
        compiler_params=_compiler_params(("arbitrary", "arbitrary"), vmem),
        name=name,
    )(h, w_stack)


def _diff_attn_kernel(q_ref, k_ref, v_ref, gate_ref, lamv_ref, subg_ref, o_ref,
                      qs_ref, vx_ref, m_ref, acc_ref, sa_ref, sb_ref, *, tq, lambda_init):
    g = pl.program_id(2)
    rows = 2 * tq
    hw = DA_V_DIM
    scale2 = (DA_HEAD_DIM ** -0.5) * LOG2E

    @pl.when(g == 0)
    def _():
        vx_ref[:, 0:hw] = v_ref[...]
        vx_ref[:, hw:2 * hw] = jnp.ones(v_ref.shape, v_ref.dtype)

    for t in range(2):
        q = q_ref[t * tq:(t + 1) * tq, :]
        lane = lax.broadcasted_iota(jnp.int32, q.shape, 1)
        zero = jnp.zeros_like(q)
        qs_ref[t * rows:t * rows + tq, :] = jnp.where(lane < DA_HEAD_DIM, q, zero)
        qs_ref[t * rows + tq:(t + 1) * rows, :] = jnp.where(lane >= DA_HEAD_DIM, q, zero)
    m_ref[...] = jnp.full(m_ref.shape, MASK_VALUE, F32)
    acc_ref[...] = jnp.zeros(acc_ref.shape, F32)

    def scores(j, s_ref, row0, nrows):
        k = k_ref[pl.ds(pl.multiple_of(j * tq, tq), tq), :]
        s_ref[0:nrows, :] = lax.dot_general(qs_ref[row0:row0 + nrows, :], k, (((1,), (1,)), ((), ())),
                                            preferred_element_type=F32)

    def update(j, s_ref, row0, nrows, diag_tile):
        s = s_ref[0:nrows, :]
        if diag_tile is not None:
            assert row0 == diag_tile * rows
            causal = (lax.broadcasted_iota(jnp.int32, (tq, tq), 1)
                      <= lax.broadcasted_iota(jnp.int32, (tq, tq), 0))
            pieces = [jnp.where(causal, s[k * tq:(k + 1) * tq, :], MASK_VALUE) for k in range(2)]
            s = jnp.concatenate(pieces + ([s[rows:nrows, :]] if nrows > rows else []), axis=0)
        m_prev = m_ref[row0:row0 + nrows, :]
        m_next = jnp.maximum(m_prev, jnp.max(s, axis=-1, keepdims=True))
        p = jnp.exp2((s - _lane_tile(m_next, tq // LANES)) * scale2)
        alpha = jnp.exp2((m_prev - m_next) * scale2)
        vx = vx_ref[pl.ds(pl.multiple_of(j * tq, tq), tq), :]
        pv = jnp.dot(p.astype(BF16), vx, preferred_element_type=F32)
        acc_ref[row0:row0 + nrows, :] = _lane_tile(alpha, 2) * acc_ref[row0:row0 + nrows, :] + pv
        m_ref[row0:row0 + nrows, :] = m_next

    lv = lamv_ref[...]
    lam = (jnp.exp(jnp.sum(lv[0:1, :] * lv[1:2, :], axis=-1, keepdims=True))
           - jnp.exp(jnp.sum(lv[2:3, :] * lv[3:4, :], axis=-1, keepdims=True)) + lambda_init)

    def finalize(t):
        o_maps = acc_ref[t * rows:(t + 1) * rows, 0:hw] / acc_ref[t * rows:(t + 1) * rows, hw:2 * hw]
        o = o_maps[0:tq, :] - lam * o_maps[tq:rows, :]
        y = _rms(o, subg_ref[...]) * (1.0 - lambda_init)
        out_rows = slice(t * tq, (t + 1) * tq)
        o_ref[out_rows, :] = (y * _silu(gate_ref[out_rows, :].astype(F32))).astype(o_ref.dtype)

    scores(0, sa_ref, 0, 2 * rows)

    def unmasked_pair(jj, carry):
        scores(2 * jj + 1, sb_ref, 0, 2 * rows)
        update(2 * jj, sa_ref, 0, 2 * rows, None)
        scores(2 * jj + 2, sa_ref, 0, 2 * rows)
        update(2 * jj + 1, sb_ref, 0, 2 * rows, None)
        return carry

    lax.fori_loop(0, g, unmasked_pair, 0)
    scores(2 * g + 1, sb_ref, rows, rows)
    update(2 * g, sa_ref, 0, 2 * rows, 0)
    finalize(0)
    update(2 * g + 1, sb_ref, rows, rows, 1)
    finalize(1)


def _diff_attention(proj, lam_vecs, subln_g, *, batch, seq, col_q, col_k, col_v, col_gate, lambda_init, tq):
    n = batch * seq
    assert seq % (2 * tq) == 0
    ng = seq // (2 * tq)
    hw = DA_V_DIM
    bq, bk, bv, bg = (_col_block(c, hw) for c in (col_q, col_k, col_v, col_gate))
    vmem = (2 * (3 * 2 * tq * hw * 2 + 2 * seq * hw * 2) + 2 * 2 * tq * hw * (2 + 12) + seq * 2 * hw * 2
            + 6 * 4 * tq * tq * 4 + (4 << 20))
    kernel = functools.partial(_diff_attn_kernel, tq=tq, lambda_init=lambda_init)
    return pl.pallas_call(
        kernel,
        grid=(batch, DA_HEADS, ng),
        in_specs=[
            pl.BlockSpec((2 * tq, hw), lambda b, h, i: (b * ng + i, bq + h)),
            pl.BlockSpec((seq, hw), lambda b, h, i: (b, bk + h)),
            pl.BlockSpec((seq, hw), lambda b, h, i: (b, bv + h)),
            pl.BlockSpec((2 * tq, hw), lambda b, h, i: (b * ng + i, bg + h)),
            pl.BlockSpec(lam_vecs.shape, lambda b, h, i: (0, 0)),
            pl.BlockSpec((1, hw), lambda b, h, i: (0, 0)),
        ],
        out_specs=pl.BlockSpec((2 * tq, hw), lambda b, h, i: (b * ng + i, h)),
        out_shape=jax.ShapeDtypeStruct((n, DA_HEADS * hw), BF16),
        scratch_shapes=[
            pltpu.VMEM((4 * tq, hw), BF16),
            pltpu.VMEM((seq, 2 * hw), BF16),
            pltpu.VMEM((4 * tq, LANES), F32),
            pltpu.VMEM((4 * tq, 2 * hw), F32),
            pltpu.VMEM((4 * tq, tq), F32),
            pltpu.VMEM((4 * tq, tq), F32),
        ],
        compiler_params=_compiler_params(("arbitrary", "arbitrary", "arbitrary"), vmem),
        name="diff_attention",
    )(proj, proj, proj, proj, lam_vecs, subln_g.reshape(1, hw))


def _gate_cols(gate_refs, cols):
    piece = gate_refs[0].shape[1]
    idx, start = divmod(cols.start, piece)
    assert cols.stop - cols.start <= piece - start
    return _silu(gate_refs[idx][:, start:start + cols.stop - cols.start].astype(F32))


def _pool_kernel(x_ref, halo_ref, gate_lo_ref, gate_hi_ref, w_ref, scale_ref, o_ref, xe_ref, la_ref, lb_ref,
                 *, ts, tiles_per_seq):
    gate_refs = (gate_lo_ref, gate_hi_ref)
    t = pl.program_id(0) % tiles_per_seq
    halo = halo_ref[...].astype(F32)
    xe_ref[0:POOL_HALO, :] = jnp.where(t == 0, jnp.zeros_like(halo), halo)
    x = x_ref[...].astype(F32)
    xe_ref[POOL_HALO:POOL_HALO + ts, :] = x
    gw = x.shape[1] // len(POOL_WINDOWS)
    pos1 = t * ts + lax.broadcasted_iota(jnp.int32, (ts, gw), 0) + 1
    for gi, w in enumerate(POOL_WINDOWS):
        cols = slice(gi * gw, (gi + 1) * gw)
        xg = x[:, cols]
        levels = w.bit_length() - 1
        src_ref, src_cols, spare = xe_ref, cols, [la_ref, lb_ref]
        for lv in range(levels):
            shift = 1 << lv
            lo = POOL_HALO if lv == levels - 1 else SUBLANES * (lv + 1)
            nrows = POOL_HALO + ts - lo
            wsum = src_ref[lo:lo + nrows, src_cols] + src_ref[lo - shift:lo - shift + nrows, src_cols]
            if lv < levels - 1:
                dst_ref = spare[lv % 2]
                dst_ref[lo:lo + nrows, :] = wsum
                src_ref, src_cols = dst_ref, slice(0, gw)
        count = jnp.minimum(pos1, w).astype(F32)
        d = (wsum / count - xg).astype(BF16)
        y = jnp.dot(d, w_ref[gi], preferred_element_type=F32) * scale_ref[:, cols]
        o_ref[:, cols] = (y * _gate_cols(gate_refs, cols)).astype(o_ref.dtype)


def _multiscale_pool(proj, pool_w, pool_scale, *, seq, col_x, col_gate, ts):
    n = proj.shape[0]
    width = pool_scale.shape[-1]
    tiles_per_seq = seq // ts
    halo_per_tile = ts // POOL_HALO
    half = width // 2
    bx, bg = _col_block(col_x, width), _col_block(col_gate, half)
    vmem = 2 * (3 * ts * width * 2 + POOL_HALO * width * 2 + pool_w.size * 2) + 8 * ts * width * 4 + (4 << 20)
    kernel = functools.partial(_pool_kernel, ts=ts, tiles_per_seq=tiles_per_seq)
    return pl.pallas_call(
        kernel,
        grid=(n // ts,),
        in_specs=[
            pl.BlockSpec((ts, width), lambda i: (i, bx)),
            pl.BlockSpec((POOL_HALO, width), lambda i: (jnp.maximum(i * halo_per_tile - 1, 0), bx)),
            pl.BlockSpec((ts, half), lambda i: (i, bg)),
            pl.BlockSpec((ts, half), lambda i: (i, bg + 1)),
            pl.BlockSpec(pool_w.shape, lambda i: (0, 0, 0)),
            pl.BlockSpec((1, width), lambda i: (0, 0)),
        ],
        out_specs=pl.BlockSpec((ts, width), lambda i: (i, 0)),
        out_shape=jax.ShapeDtypeStruct((n, width), BF16),
        scratch_shapes=[pltpu.VMEM((POOL_HALO + ts, width), F32)]
        + [pltpu.VMEM((POOL_HALO + ts, width // len(POOL_WINDOWS)), F32)] * 2,
        compiler_params=_compiler_params(("arbitrary",), vmem),
        name="multiscale_pool",
    )(proj, proj, proj, proj, pool_w, pool_scale.reshape(1, width))


def _proj_lru_kernel(h_ref, w_ref, x_ref, gate_ref, cw_ref, cb_ref, wax_ref, ba_ref, bx_ref, lam_ref,
                     o_ref, y_ref, wb_ref, xe_ref, carry_ref, a_ref, u_ref, *, tiles_per_seq):
    i = pl.program_id(1)
    t = i % tiles_per_seq
    ts, bw = x_ref.shape

    @pl.when(i == 0)
    def _():
        wb_ref[...] = w_ref[...].astype(BF16)

    @pl.when(t == 0)
    def _():
        xe_ref[0:CONV_HALO, :] = jnp.zeros((CONV_HALO, bw), F32)
        carry_ref[...] = jnp.zeros(carry_ref.shape, F32)

    @pl.when(t > 0)
    def _():
        xe_ref[0:CONV_HALO, :] = xe_ref[ts:ts + CONV_HALO, :]

    xe_ref[CONV_HALO:CONV_HALO + ts, :] = x_ref[...].astype(F32)

    xc = cb_ref[...] + cw_ref[CONV_WIDTH - 1:CONV_WIDTH, :] * xe_ref[CONV_HALO:CONV_HALO + ts, :]
    for back in range(1, CONV_WIDTH):
        tap = CONV_WIDTH - 1 - back
        xc = xc + cw_ref[tap:tap + 1, :] * xe_ref[CONV_HALO - back:CONV_HALO - back + ts, :]

    neg_lam = -lam_ref[...]
    softplus = jnp.maximum(neg_lam, 0.0) + jnp.log(1.0 + jnp.exp(-jnp.abs(neg_lam)))
    z = jnp.dot(xc.astype(BF16), wax_ref[...], preferred_element_type=F32)
    o_ref[...] = jnp.dot(h_ref[...], wb_ref[...], preferred_element_type=F32).astype(o_ref.dtype)

    r = _sigmoid(z[:, 0:bw] + ba_ref[...])
    gate_i = _sigmoid(z[:, bw:2 * bw] + bx_ref[...])
    log_a = (-LRU_C) * r * softplus
    th = jnp.tanh(log_a)
    m2 = -2.0 * th / (1.0 - th)
    mult = jnp.where(m2 > 0.0, m2 * lax.rsqrt(m2), 0.0)
    row = lax.broadcasted_iota(jnp.int32, (ts, bw), 0)
    first_row = jnp.where(t == 0, 0, -1)
    mult = jnp.where(row == first_row, 1.0, mult)
    a_all = jnp.exp(log_a)
    u_all = mult * (gate_i * xc)

    seg = ts // SUBLANES
    pitch = _segment_pitch(seg)
    for i in range(SUBLANES):
        a_ref[i * pitch:i * pitch + seg, :] = a_all[i * seg:(i + 1) * seg, :]
        u_ref[i * pitch:i * pitch + seg, :] = u_all[i * seg:(i + 1) * seg, :]
    step = lambda ref, j: ref[pl.ds(j, SUBLANES, stride=pitch), :]
    h_end = jnp.zeros((SUBLANES, bw), F32)
    a_end = jnp.ones((SUBLANES, bw), F32)
    for j in range(seg):
        a_j = step(a_ref, j)
        h_end = a_j * h_end + step(u_ref, j)
        a_end = a_j * a_end
    sub = lax.broadcasted_iota(jnp.int32, (SUBLANES, bw), 0)
    for k in (1, 2, 4):
        keep = sub >= k
        a_prev = jnp.where(keep, pltpu.roll(a_end, k, 0), 1.0)
        h_prev = jnp.where(keep, pltpu.roll(h_end, k, 0), 0.0)
        h_end = a_end * h_prev + h_end
        a_end = a_end * a_prev
    carry_in = carry_ref[...]
    state = a_end * carry_in + h_end
    carry_ref[...] = jnp.broadcast_to(state[SUBLANES - 1:SUBLANES, :], state.shape)
    h = jnp.where(sub >= 1, pltpu.roll(state, 1, 0), carry_in)
    for j in range(seg):
        h = step(a_ref, j) * h + step(u_ref, j)
        u_ref[pl.ds(j, SUBLANES, stride=pitch), :] = h
    hs = jnp.concatenate([u_ref[i * pitch:i * pitch + seg, :] for i in range(SUBLANES)], axis=0)
    y_ref[...] = (hs * _silu(gate_ref[...].astype(F32))).astype(y_ref.dtype)


def _proj_lru(h, w_stack, layer, proj_a, conv_w, conv_b, w_ax, b_a, b_x, lam, *, seq, panel0, col_x, col_gate,
              tm, tn):
    n, d = h.shape
    assert w_stack.shape[2] == (panel0 + LRU_BLOCKS) * tn and seq % tm == 0
    width = conv_b.shape[-1]
    bw = width // LRU_BLOCKS
    bx, bg = _col_block(col_x, bw), _col_block(col_gate, bw)
    vec = lambda a: a.reshape(1, width)
    vmem = (2 * (tm * d * 2 + d * tn * 4 + tm * tn * 2) + d * tn * 2 + tm * tn * 4
            + 24 * tm * bw * 4 + (4 << 20))
    kernel = functools.partial(_proj_lru_kernel, tiles_per_seq=seq // tm)
    vec_spec = pl.BlockSpec((1, bw), lambda p, i: (0, p))
    return pl.pallas_call(
        kernel,
        grid=(LRU_BLOCKS, n // tm),
        in_specs=[
            pl.BlockSpec((tm, d), lambda p, i: (i, 0)),
            pl.BlockSpec((None, d, tn), lambda p, i: (layer, 0, panel0 + p)),
            pl.BlockSpec((tm, bw), lambda p, i: (i, bx + p)),
            pl.BlockSpec((tm, bw), lambda p, i: (i, bg + p)),
            pl.BlockSpec((CONV_WIDTH, bw), lambda p, i: (0, p)),
            vec_spec,
            pl.BlockSpec((None, bw, 2 * bw), lambda p, i: (p, 0, 0)),
            vec_spec, vec_spec, vec_spec,
        ],
        out_specs=[
            pl.BlockSpec((tm, tn), lambda p, i: (i, p)),
            pl.BlockSpec((tm, bw), lambda p, i: (i, p)),
        ],
        out_shape=[jax.ShapeDtypeStruct((n, LRU_BLOCKS * tn), BF16), jax.ShapeDtypeStruct((n, width), BF16)],
        scratch_shapes=[
            pltpu.VMEM((d, tn), BF16),
            pltpu.VMEM((CONV_HALO + tm, bw), F32),
            pltpu.VMEM((SUBLANES, bw), F32),
            pltpu.VMEM((SUBLANES * _segment_pitch(tm // SUBLANES), bw), F32),
            pltpu.VMEM((SUBLANES * _segment_pitch(tm // SUBLANES), bw), F32),
        ],
        compiler_params=_compiler_params(("arbitrary", "arbitrary"), vmem),
        name="in_proj_lru",
    )(h, w_stack, proj_a, proj_a, conv_w, vec(conv_b), w_ax, vec(b_a), vec(b_x), vec(lam))


def _mem_attn_kernel(q_ref, kv_ref, gate_ref, o_ref):
    scale = MEM_HEAD_DIM ** -0.5
    kv_width = MEM_HEADS * MEM_HEAD_DIM
    for hd in range(MEM_HEADS):
        cols = slice(hd * MEM_HEAD_DIM, (hd + 1) * MEM_HEAD_DIM)
        vcols = slice(kv_width + hd * MEM_HEAD_DIM, kv_width + (hd + 1) * MEM_HEAD_DIM)
        s = lax.dot_general(q_ref[:, cols], kv_ref[:, cols], (((1,), (1,)), ((), ())),
                            preferred_element_type=F32) * scale
        p = jnp.exp(s - jnp.max(s, axis=-1, keepdims=True))
        l = jnp.sum(p, axis=-1, keepdims=True)
        o = jnp.dot(p.astype(BF16), kv_ref[:, vcols], preferred_element_type=F32) / l
        o_ref[:, cols] = (o * _silu(gate_ref[:, cols].astype(F32))).astype(o_ref.dtype)


def _memory_attention(proj, mkv, *, seq, mem_len, col_q, col_gate, tq):
    n = proj.shape[0]
    width = MEM_HEADS * MEM_HEAD_DIM
    tiles_per_seq = seq // tq
    bq, bg = _col_block(col_q, width), _col_block(col_gate, width)
    vmem = 2 * (3 * tq * width * 2 + mem_len * 2 * width * 2) + 8 * tq * mem_len * 4 + (4 << 20)
    return pl.pallas_call(
        _mem_attn_kernel,
        grid=(n // tq,),
        in_specs=[
            pl.BlockSpec((tq, width), lambda i: (i, bq)),
            pl.BlockSpec((mem_len, 2 * width), lambda i: (i // tiles_per_seq, 0)),
            pl.BlockSpec((tq, width), lambda i: (i, bg)),
        ],
        out_specs=pl.BlockSpec((tq, width), lambda i: (i, 0)),
        out_shape=jax.ShapeDtypeStruct((n, width), BF16),
        compiler_params=_compiler_params(("arbitrary",), vmem),
        name="memory_attention",
    )(proj, mkv, proj)


def _merge_kernel(*refs):
    nb = (len(refs) - 1) // 3
    y_refs, w_refs, g_refs, o_ref = refs[:nb], refs[nb:2 * nb], refs[2 * nb:3 * nb], refs[-1]
    acc = None
    for y_ref, w_ref, g_ref in zip(y_refs, w_refs, g_refs):
        term = _sigmoid(g_ref[...].astype(F32)) * jnp.dot(y_ref[...], w_ref[...], preferred_element_type=F32)
        acc = term if acc is None else acc + term
    o_ref[...] = acc.astype(o_ref.dtype)


def _merge(branches, gates, w_branch, *, col_merge, tm, tn):
    n = gates.shape[0]
    d = w_branch.shape[1]
    sizes = [y.shape[1] for y in branches]
    offsets = [sum(sizes[:i]) for i in range(len(sizes))]
    y_specs = [pl.BlockSpec((tm, sz), lambda i, j: (i, 0)) for sz in sizes]
    w_specs = [pl.BlockSpec((sz, tn), lambda i, j, r=_col_block(off, sz): (r, j)) for sz, off in zip(sizes, offsets)]
    g_specs = [pl.BlockSpec((tm, tn), lambda i, j, c=_col_block(col_merge + bi * d, tn): (i, c + j))
               for bi in range(len(sizes))]
    total = sum(sizes)
    vmem = 2 * (tm * total * 2 + total * tn * 2 + len(sizes) * tm * tn * 2 + tm * tn * 2) + 4 * tm * tn * 4 + (4 << 20)
    return pl.pallas_call(
        _merge_kernel,
        grid=(n // tm, d // tn),
        in_specs=y_specs + w_specs + g_specs,
        out_specs=pl.BlockSpec((tm, tn), lambda i, j: (i, j)),
        out_shape=jax.ShapeDtypeStruct((n, d), BF16),
        compiler_params=_compiler_params(("arbitrary", "arbitrary"), vmem),
        name="branch_merge",
    )(*branches, *([w_branch] * len(sizes)), *([gates] * len(sizes)))


def _out_proj_kernel(x_ref, m_ref, w_ref, g_ref, *o_refs, last):
    y = x_ref[...] + jnp.dot(m_ref[...], w_ref[...], preferred_element_type=F32)
    normed = _rms(y, g_ref[...])
    if last:
        o_refs[0][...] = normed
    else:
        o_refs[0][...] = y
        o_refs[1][...] = normed.astype(BF16)


def _out_proj(x, merged, w_out, g, *, last, tm):
    n, d = x.shape
    vmem = 2 * (2 * tm * d * 4 + 2 * tm * d * 2 + d * d * 2) + 3 * tm * d * 4 + (4 << 20)
    row_spec = pl.BlockSpec((tm, d), lambda i: (i, 0))
    out_shape = [jax.ShapeDtypeStruct((n, d), F32)] + ([] if last else [jax.ShapeDtypeStruct((n, d), BF16)])
    return pl.pallas_call(
        functools.partial(_out_proj_kernel, last=last),
        grid=(n // tm,),
        in_specs=[row_spec, row_spec, pl.BlockSpec((d, d), lambda i: (0, 0)), pl.BlockSpec((1, d), lambda i: (0, 0))],
        out_specs=[row_spec] * len(out_shape),
        out_shape=out_shape,
        compiler_params=_compiler_params(("arbitrary",), vmem),
        name="out_proj",
    )(x, merged, w_out, g.reshape(1, d))


def kernel(x, mem, norm_g, w_in, lam_vecs, subln_g, pool_w, pool_scale, conv_w, conv_b, lru_wa, lru_ba,
           lru_wx, lru_bx, lru_lambda, mem_norm_g, w_mem_kv, w_branch, w_out, final_g):
    batch, seq, d = x.shape
    mem_len = mem.shape[1]
    depth = w_in.shape[0]
    da_width = DA_HEADS * DA_V_DIM
    pool_width = pool_scale.shape[-1]
    lru_width = conv_b.shape[-1]
    mem_width = MEM_HEADS * MEM_HEAD_DIM
    branch_width = da_width + pool_width + lru_width + mem_width
    col_q = 0
    col_k = col_q + da_width
    col_v = col_k + da_width
    col_pool = col_v + da_width
    col_lru = col_pool + pool_width
    col_qmem = col_lru + lru_width
    col_gate = col_qmem + mem_width
    col_merge = col_gate + branch_width
    gate_a = col_gate
    gate_b = gate_a + da_width
    gate_c = gate_b + pool_width
    gate_m = gate_c + lru_width

    xf = x.reshape(batch * seq, d)
    memf = mem.reshape(batch * mem_len, d)
    h = _norm(xf, norm_g[0], tm=1024)
    for l in range(depth):
        last = l == depth - 1
        lambda_init = 0.8 - 0.6 * math.exp(-0.3 * l)
        proj = _proj(h, w_in, l, panels=_col_block(col_merge, PROJ_TN), tm=2048, tn=PROJ_TN, name="in_proj")
        w_ax = jnp.concatenate([lru_wa[l], lru_wx[l]], axis=-1).astype(BF16)
        merge_gates, y_c = _proj_lru(h, w_in, l, proj, conv_w[l], conv_b[l], w_ax, lru_ba[l], lru_bx[l],
                                     lru_lambda[l], seq=seq, panel0=_col_block(col_merge, PROJ_TN),
                                     col_x=col_lru, col_gate=gate_c, tm=1024, tn=PROJ_TN)
        mkv = _norm_proj(memf, mem_norm_g[l], w_mem_kv[l].astype(BF16), tm=batch * mem_len, tn=512, name="mem_kv")
        y_a = _diff_attention(proj, lam_vecs[l], subln_g[l], batch=batch, seq=seq, col_q=col_q, col_k=col_k,
                              col_v=col_v, col_gate=gate_a, lambda_init=lambda_init, tq=512)
        y_b = _multiscale_pool(proj, pool_w[l].astype(BF16), pool_scale[l], seq=seq, col_x=col_pool,
                               col_gate=gate_b, ts=1024)
        y_m = _memory_attention(proj, mkv, seq=seq, mem_len=mem_len, col_q=col_qmem, col_gate=gate_m, tq=1024)
        merged = _merge([y_a, y_b, y_c, y_m], merge_gates, w_branch[l].astype(BF16), col_merge=0, tm=1024, tn=512)
        if last:
            (out,) = _out_proj(xf, merged, w_out[l].astype(BF16), final_g, last=True, tm=512)
        else:
            xf, h = _out_proj(xf, merged, w_out[l].astype(BF16), norm_g[l + 1], last=False, tm=512)
    return out.reshape(batch, seq, d)
```

```python
import functools
import math

import jax
import jax.numpy as jnp
from jax import lax
from jax.experimental import pallas as pl
from jax.experimental.pallas import tpu as pltpu

F32 = jnp.float32
BF16 = jnp.bfloat16

EPS = 1e-6
DA_HEADS = 8
DA_HEAD_DIM = 64
DA_V_DIM = 2 * DA_HEAD_DIM
POOL_WINDOWS = (2, 4, 8, 16)
POOL_HALO = 32
LRU_BLOCKS = 8
CONV_WIDTH = 4
CONV_HALO = 8
LRU_C = 8.0
MEM_HEADS = 4
MEM_HEAD_DIM = 128
LANES = 128
SUBLANES = 8
V7X_VMEM_BYTES = 64 * 1024 * 1024
PROJ_TN = 1024
MASK_VALUE = -1e30
LOG2E = 1.4426950408889634


def _compiler_params(semantics, vmem_bytes):
    assert vmem_bytes < V7X_VMEM_BYTES
    return pltpu.CompilerParams(dimension_semantics=semantics, vmem_limit_bytes=int(vmem_bytes))


def _col_block(col, width):
    assert col % width == 0, (col, width)
    return col // width


def _rms(x, g):
    ms = jnp.mean(x * x, axis=-1, keepdims=True)
    return x * lax.rsqrt(ms + EPS) * g


def _segment_pitch(seg):
    return seg + (4 - seg) % 8


def _lane_tile(x, n):
    return jnp.concatenate([x] * n, axis=1)


def _sigmoid(x):
    return 0.5 * jnp.tanh(0.5 * x) + 0.5


def _silu(x):
    return x * _sigmoid(x)


def _norm_proj_kernel(x_ref, g_ref, w_ref, o_ref, h_ref):
    @pl.when(pl.program_id(1) == 0)
    def _():
        h_ref[...] = _rms(x_ref[...], g_ref[...]).astype(BF16)

    o_ref[...] = jnp.dot(h_ref[...], w_ref[...], preferred_element_type=F32).astype(o_ref.dtype)


def _norm_proj(x, g, w, *, tm, tn, name):
    n, d = x.shape
    nc = w.shape[1]
    vmem = 2 * (tm * d * 4 + d * tn * 2 + tm * tn * 2) + tm * d * 2 + (4 << 20)
    return pl.pallas_call(
        _norm_proj_kernel,
        grid=(n // tm, nc // tn),
        in_specs=[
            pl.BlockSpec((tm, d), lambda i, j: (i, 0)),
            pl.BlockSpec((1, d), lambda i, j: (0, 0)),
            pl.BlockSpec((d, tn), lambda i, j: (0, j)),
        ],
        out_specs=pl.BlockSpec((tm, tn), lambda i, j: (i, j)),
        out_shape=jax.ShapeDtypeStruct((n, nc), BF16),
        scratch_shapes=[pltpu.VMEM((tm, d), BF16)],
        compiler_params=_compiler_params(("arbitrary", "arbitrary"), vmem),
        name=name,
    )(x, g.reshape(1, d), w)


def _norm_kernel(x_ref, g_ref, o_ref):
    o_ref[...] = _rms(x_ref[...], g_ref[...]).astype(o_ref.dtype)


def _norm(x, g, *, tm):
    n, d = x.shape
    vmem = 2 * (tm * d * 4 + tm * d * 2) + 3 * tm * d * 4 + (4 << 20)
    return pl.pallas_call(
        _norm_kernel,
        grid=(n // tm,),
        in_specs=[pl.BlockSpec((tm, d), lambda i: (i, 0)), pl.BlockSpec((1, d), lambda i: (0, 0))],
        out_specs=pl.BlockSpec((tm, d), lambda i: (i, 0)),
        out_shape=jax.ShapeDtypeStruct((n, d), BF16),
        compiler_params=_compiler_params(("arbitrary",), vmem),
        name="rms_norm",
    )(x, g.reshape(1, d))


def _proj_kernel(h_ref, w_ref, o_ref, wb_ref):
    @pl.when(pl.program_id(1) == 0)
    def _():
        wb_ref[...] = w_ref[...].astype(BF16)

    o_ref[...] = jnp.dot(h_ref[...], wb_ref[...], preferred_element_type=F32).astype(o_ref.dtype)


def _proj(h, w_stack, layer, *, panels, tm, tn, name):
    n, d = h.shape
    vmem = 2 * (tm * d * 2 + d * tn * 4 + tm * tn * 2) + d * tn * 2 + tm * tn * 4 + (4 << 20)
    return pl.pallas_call(
        _proj_kernel,
        grid=(panels, n // tm),
        in_specs=[
            pl.BlockSpec((tm, d), lambda j, i: (i, 0)),
            pl.BlockSpec((None, d, tn), lambda j, i: (layer, 0, j)),
        ],
        out_specs=pl.BlockSpec((tm, tn), lambda j, i: (i, j)),
        out_shape=jax.ShapeDtypeStruct((n, panels * tn), BF16),
        scratch_shapes=[pltpu.VMEM((d, tn), BF16)],
        compiler_params=_compiler_params(("arbitrary", "arbitrary"), vmem),
        name=name,
    )(h, w_stack)


def _diff_attn_kernel(q_ref, k_ref, v_ref, gate_ref, lamv_ref, subg_ref, o_ref,
                      qs_ref, vx_ref, m_ref, acc_ref, sa_ref, sb_ref, *, tq, lambda_init):
    g = pl.program_id(2)
    rows = 2 * tq
    hw = DA_V_DIM
    scale2 = (DA_HEAD_DIM ** -0.5) * LOG2E

    @pl.when(g == 0)
    def _():
        vx_ref[:, 0:hw] = v_ref[...]
        vx_ref[:, hw:2 * hw] = jnp.ones(v_ref.shape, v_ref.dtype)

    for t in range(2):
        q = q_ref[t * tq:(t + 1) * tq, :]
        lane = lax.broadcasted_iota(jnp.int32, q.shape, 1)
        zero = jnp.zeros_like(q)
        qs_ref[t * rows:t * rows + tq, :] = jnp.where(lane < DA_HEAD_DIM, q, zero)
        qs_ref[t * rows + tq:(t + 1) * rows, :] = jnp.where(lane >= DA_HEAD_DIM, q, zero)
    m_ref[...] = jnp.full(m_ref.shape, MASK_VALUE, F32)
    acc_ref[...] = jnp.zeros(acc_ref.shape, F32)

    def scores(j, s_ref, row0, nrows):
        k = k_ref[pl.ds(pl.multiple_of(j * tq, tq), tq), :]
        s_ref[0:nrows, :] = lax.dot_general(qs_ref[row0:row0 + nrows, :], k, (((1,), (1,)), ((), ())),
                                            preferred_element_type=F32)

    def update(j, s_ref, row0, nrows, diag_tile):
        s = s_ref[0:nrows, :]
        if diag_tile is not None:
            assert row0 == diag_tile * rows
            causal = (lax.broadcasted_iota(jnp.int32, (tq, tq), 1)
                      <= lax.broadcasted_iota(jnp.int32, (tq, tq), 0))
            pieces = [jnp.where(causal, s[k * tq:(k + 1) * tq, :], MASK_VALUE) for k in range(2)]
            s = jnp.concatenate(pieces + ([s[rows:nrows, :]] if nrows > rows else []), axis=0)
        m_prev = m_ref[row0:row0 + nrows, :]
        m_next = jnp.maximum(m_prev, jnp.max(s, axis=-1, keepdims=True))
        p = jnp.exp2((s - _lane_tile(m_next, tq // LANES)) * scale2)
        alpha = jnp.exp2((m_prev - m_next) * scale2)
        vx = vx_ref[pl.ds(pl.multiple_of(j * tq, tq), tq), :]
        pv = jnp.dot(p.astype(BF16), vx, preferred_element_type=F32)
        acc_ref[row0:row0 + nrows, :] = _lane_tile(alpha, 2) * acc_ref[row0:row0 + nrows, :] + pv
        m_ref[row0:row0 + nrows, :] = m_next

    lv = lamv_ref[...]
    lam = (jnp.exp(jnp.sum(lv[0:1, :] * lv[1:2, :], axis=-1, keepdims=True))
           - jnp.exp(jnp.sum(lv[2:3, :] * lv[3:4, :], axis=-1, keepdims=True)) + lambda_init)

    def finalize(t):
        o_maps = acc_ref[t * rows:(t + 1) * rows, 0:hw] / acc_ref[t * rows:(t + 1) * rows, hw:2 * hw]
        o = o_maps[0:tq, :] - lam * o_maps[tq:rows, :]
        y = _rms(o, subg_ref[...]) * (1.0 - lambda_init)
        out_rows = slice(t * tq, (t + 1) * tq)
        o_ref[out_rows, :] = (y * _silu(gate_ref[out_rows, :].astype(F32))).astype(o_ref.dtype)

    scores(0, sa_ref, 0, 2 * rows)

    def unmasked_pair(jj, carry):
        scores(2 * jj + 1, sb_ref, 0, 2 * rows)
        update(2 * jj, sa_ref, 0, 2 * rows, None)
        scores(2 * jj + 2, sa_ref, 0, 2 * rows)
        update(2 * jj + 1, sb_ref, 0, 2 * rows, None)
        return carry

    lax.fori_loop(0, g, unmasked_pair, 0)
    scores(2 * g + 1, sb_ref, rows, rows)
    update(2 * g, sa_ref, 0, 2 * rows, 0)
    finalize(0)
    update(2 * g + 1, sb_ref, rows, rows, 1)
    finalize(1)


def _diff_attention(proj, lam_vecs, subln_g, *, batch, seq, col_q, col_k, col_v, col_gate, lambda_init, tq):
    n = batch * seq
    assert seq % (2 * tq) == 0
    ng = seq // (2 * tq)
    hw = DA_V_DIM
    bq, bk, bv, bg = (_col_block(c, hw) for c in (col_q, col_k, col_v, col_gate))
    vmem = (2 * (3 * 2 * tq * hw * 2 + 2 * seq * hw * 2) + 2 * 2 * tq * hw * (2 + 12) + seq * 2 * hw * 2
            + 6 * 4 * tq * tq * 4 + (4 << 20))
    kernel = functools.partial(_diff_attn_kernel, tq=tq, lambda_init=lambda_init)
    return pl.pallas_call(
        kernel,
        grid=(batch, DA_HEADS, ng),
        in_specs=[
            pl.BlockSpec((2 * tq, hw), lambda b, h, i: (b * ng + i, bq + h)),
            pl.BlockSpec((seq, hw), lambda b, h, i: (b, bk + h)),
            pl.BlockSpec((seq, hw), lambda b, h, i: (b, bv + h)),
            pl.BlockSpec((2 * tq, hw), lambda b, h, i: (b * ng + i, bg + h)),
            pl.BlockSpec(lam_vecs.shape, lambda b, h, i: (0, 0)),
            pl.BlockSpec((1, hw), lambda b, h, i: (0, 0)),
        ],
        out_specs=pl.BlockSpec((2 * tq, hw), lambda b, h, i: (b * ng + i, h)),
        out_shape=jax.ShapeDtypeStruct((n, DA_HEADS * hw), BF16),
        scratch_shapes=[
            pltpu.VMEM((4 * tq, hw), BF16),
            pltpu.VMEM((seq, 2 * hw), BF16),
            pltpu.VMEM((4 * tq, LANES), F32),
            pltpu.VMEM((4 * tq, 2 * hw), F32),
            pltpu.VMEM((4 * tq, tq), F32),
            pltpu.VMEM((4 * tq, tq), F32),
        ],
        compiler_params=_compiler_params(("arbitrary", "arbitrary", "arbitrary"), vmem),
        name="diff_attention",
    )(proj, proj, proj, proj, lam_vecs, subln_g.reshape(1, hw))


def _gate_cols(gate_refs, cols):
    piece = gate_refs[0].shape[1]
    idx, start = divmod(cols.start, piece)
    assert cols.stop - cols.start <= piece - start
    return _silu(gate_refs[idx][:, start:start + cols.stop - cols.start].astype(F32))


def _pool_kernel(x_ref, halo_ref, gate_lo_ref, gate_hi_ref, w_ref, scale_ref, o_ref, xe_ref, la_ref, lb_ref,
                 *, ts, tiles_per_seq):
    gate_refs = (gate_lo_ref, gate_hi_ref)
    t = pl.program_id(0) % tiles_per_seq
    halo = halo_ref[...].astype(F32)
    xe_ref[0:POOL_HALO, :] = jnp.where(t == 0, jnp.zeros_like(halo), halo)
    x = x_ref[...].astype(F32)
    xe_ref[POOL_HALO:POOL_HALO + ts, :] = x
    gw = x.shape[1] // len(POOL_WINDOWS)
    pos1 = t * ts + lax.broadcasted_iota(jnp.int32, (ts, gw), 0) + 1
    for gi, w in enumerate(POOL_WINDOWS):
        cols = slice(gi * gw, (gi + 1) * gw)
        xg = x[:, cols]
        levels = w.bit_length() - 1
        src_ref, src_cols, spare = xe_ref, cols, [la_ref, lb_ref]
        for lv in range(levels):
            shift = 1 << lv
            lo = POOL_HALO if lv == levels - 1 else SUBLANES * (lv + 1)
            nrows = POOL_HALO + ts - lo
            wsum = src_ref[lo:lo + nrows, src_cols] + src_ref[lo - shift:lo - shift + nrows, src_cols]
            if lv < levels - 1:
                dst_ref = spare[lv % 2]
                dst_ref[lo:lo + nrows, :] = wsum
                src_ref, src_cols = dst_ref, slice(0, gw)
        count = jnp.minimum(pos1, w).astype(F32)
        d = (wsum / count - xg).astype(BF16)
        y = jnp.dot(d, w_ref[gi], preferred_element_type=F32) * scale_ref[:, cols]
        o_ref[:, cols] = (y * _gate_cols(gate_refs, cols)).astype(o_ref.dtype)


def _multiscale_pool(proj, pool_w, pool_scale, *, seq, col_x, col_gate, ts):
    n = proj.shape[0]
    width = pool_scale.shape[-1]
    tiles_per_seq = seq // ts
    halo_per_tile = ts // POOL_HALO
    half = width // 2
    bx, bg = _col_block(col_x, width), _col_block(col_gate, half)
    vmem = 2 * (3 * ts * width * 2 + POOL_HALO * width * 2 + pool_w.size * 2) + 8 * ts * width * 4 + (4 << 20)
    kernel = functools.partial(_pool_kernel, ts=ts, tiles_per_seq=tiles_per_seq)
    return pl.pallas_call(
        kernel,
        grid=(n // ts,),
        in_specs=[
            pl.BlockSpec((ts, width), lambda i: (i, bx)),
            pl.BlockSpec((POOL_HALO, width), lambda i: (jnp.maximum(i * halo_per_tile - 1, 0), bx)),
            pl.BlockSpec((ts, half), lambda i: (i, bg)),
            pl.BlockSpec((ts, half), lambda i: (i, bg + 1)),
            pl.BlockSpec(pool_w.shape, lambda i: (0, 0, 0)),
            pl.BlockSpec((1, width), lambda i: (0, 0)),
        ],
        out_specs=pl.BlockSpec((ts, width), lambda i: (i, 0)),
        out_shape=jax.ShapeDtypeStruct((n, width), BF16),
        scratch_shapes=[pltpu.VMEM((POOL_HALO + ts, width), F32)]
        + [pltpu.VMEM((POOL_HALO + ts, width // len(POOL_WINDOWS)), F32)] * 2,
        compiler_params=_compiler_params(("arbitrary",), vmem),
        name="multiscale_pool",
    )(proj, proj, proj, proj, pool_w, pool_scale.reshape(1, width))


def _proj_lru_kernel(h_ref, w_ref, x_ref, gate_ref, cw_ref, cb_ref, wax_ref, ba_ref, bx_ref, lam_ref,
                     o_ref, y_ref, wb_ref, xe_ref, carry_ref, a_ref, u_ref, *, tiles_per_seq):
    i = pl.program_id(1)
    t = i % tiles_per_seq
    ts, bw = x_ref.shape

    @pl.when(i == 0)
    def _():
        wb_ref[...] = w_ref[...].astype(BF16)

    @pl.when(t == 0)
    def _():
        xe_ref[0:CONV_HALO, :] = jnp.zeros((CONV_HALO, bw), F32)
        carry_ref[...] = jnp.zeros(carry_ref.shape, F32)

    @pl.when(t > 0)
    def _():
        xe_ref[0:CONV_HALO, :] = xe_ref[ts:ts + CONV_HALO, :]

    xe_ref[CONV_HALO:CONV_HALO + ts, :] = x_ref[...].astype(F32)

    xc = cb_ref[...] + cw_ref[CONV_WIDTH - 1:CONV_WIDTH, :] * xe_ref[CONV_HALO:CONV_HALO + ts, :]
    for back in range(1, CONV_WIDTH):
        tap = CONV_WIDTH - 1 - back
        xc = xc + cw_ref[tap:tap + 1, :] * xe_ref[CONV_HALO - back:CONV_HALO - back + ts, :]

    neg_lam = -lam_ref[...]
    softplus = jnp.maximum(neg_lam, 0.0) + jnp.log(1.0 + jnp.exp(-jnp.abs(neg_lam)))
    z = jnp.dot(xc.astype(BF16), wax_ref[...], preferred_element_type=F32)
    o_ref[...] = jnp.dot(h_ref[...], wb_ref[...], preferred_element_type=F32).astype(o_ref.dtype)

    r = _sigmoid(z[:, 0:bw] + ba_ref[...])
    gate_i = _sigmoid(z[:, bw:2 * bw] + bx_ref[...])
    log_a = (-LRU_C) * r * softplus
    th = jnp.tanh(log_a)
    m2 = -2.0 * th / (1.0 - th)
    mult = jnp.where(m2 > 0.0, m2 * lax.rsqrt(m2), 0.0)
    row = lax.broadcasted_iota(jnp.int32, (ts, bw), 0)
    first_row = jnp.where(t == 0, 0, -1)
    mult = jnp.where(row == first_row, 1.0, mult)
    a_all = jnp.exp(log_a)
    u_all = mult * (gate_i * xc)

    seg = ts // SUBLANES
    pitch = _segment_pitch(seg)
    for i in range(SUBLANES):
        a_ref[i * pitch:i * pitch + seg, :] = a_all[i * seg:(i + 1) * seg, :]
        u_ref[i * pitch:i * pitch + seg, :] = u_all[i * seg:(i + 1) * seg, :]
    step = lambda ref, j: ref[pl.ds(j, SUBLANES, stride=pitch), :]
    h_end = jnp.zeros((SUBLANES, bw), F32)
    a_end = jnp.ones((SUBLANES, bw), F32)
    for j in range(seg):
        a_j = step(a_ref, j)
        h_end = a_j * h_end + step(u_ref, j)
        a_end = a_j * a_end
    sub = lax.broadcasted_iota(jnp.int32, (SUBLANES, bw), 0)
    for k in (1, 2, 4):
        keep = sub >= k
        a_prev = jnp.where(keep, pltpu.roll(a_end, k, 0), 1.0)
        h_prev = jnp.where(keep, pltpu.roll(h_end, k, 0), 0.0)
        h_end = a_end * h_prev + h_end
        a_end = a_end * a_prev
    carry_in = carry_ref[...]
    state = a_end * carry_in + h_end
    carry_ref[...] = jnp.broadcast_to(state[SUBLANES - 1:SUBLANES, :], state.shape)
    h = jnp.where(sub >= 1, pltpu.roll(state, 1, 0), carry_in)
    for j in range(seg):
        h = step(a_ref, j) * h + step(u_ref, j)
        u_ref[pl.ds(j, SUBLANES, stride=pitch), :] = h
    hs = jnp.concatenate([u_ref[i * pitch:i * pitch + seg, :] for i in range(SUBLANES)], axis=0)
    y_ref[...] = (hs * _silu(gate_ref[...].astype(F32))).astype(y_ref.dtype)


def _proj_lru(h, w_stack, layer, proj_a, conv_w, conv_b, w_ax, b_a, b_x, lam, *, seq, panel0, col_x, col_gate,
              tm, tn):
    n, d = h.shape
    assert w_stack.shape[2] == (panel0 + LRU_BLOCKS) * tn and seq % tm == 0
    width = conv_b.shape[-1]
    bw = width // LRU_BLOCKS
    bx, bg = _col_block(col_x, bw), _col_block(col_gate, bw)
    vec = lambda a: a.reshape(1, width)
    vmem = (2 * (tm * d * 2 + d * tn * 4 + tm * tn * 2) + d * tn * 2 + tm * tn * 4
            + 24 * tm * bw * 4 + (4 << 20))
    kernel = functools.partial(_proj_lru_kernel, tiles_per_seq=seq // tm)
    vec_spec = pl.BlockSpec((1, bw), lambda p, i: (0, p))
    return pl.pallas_call(
        kernel,
        grid=(LRU_BLOCKS, n // tm),
        in_specs=[
            pl.BlockSpec((tm, d), lambda p, i: (i, 0)),
            pl.BlockSpec((None, d, tn), lambda p, i: (layer, 0, panel0 + p)),
            pl.BlockSpec((tm, bw), lambda p, i: (i, bx + p)),
            pl.BlockSpec((tm, bw), lambda p, i: (i, bg + p)),
            pl.BlockSpec((CONV_WIDTH, bw), lambda p, i: (0, p)),
            vec_spec,
            pl.BlockSpec((None, bw, 2 * bw), lambda p, i: (p, 0, 0)),
            vec_spec, vec_spec, vec_spec,
        ],
        out_specs=[
            pl.BlockSpec((tm, tn), lambda p, i: (i, p)),
            pl.BlockSpec((tm, bw), lambda p, i: (i, p)),
        ],
        out_shape=[jax.ShapeDtypeStruct((n, LRU_BLOCKS * tn), BF16), jax.ShapeDtypeStruct((n, width), BF16)],
        scratch_shapes=[
            pltpu.VMEM((d, tn), BF16),
            pltpu.VMEM((CONV_HALO + tm, bw), F32),
            pltpu.VMEM((SUBLANES, bw), F32),
            pltpu.VMEM((SUBLANES * _segment_pitch(tm // SUBLANES), bw), F32),
            pltpu.VMEM((SUBLANES * _segment_pitch(tm // SUBLANES), bw), F32),
        ],
        compiler_params=_compiler_params(("arbitrary", "arbitrary"), vmem),
        name="in_proj_lru",
    )(h, w_stack, proj_a, proj_a, conv_w, vec(conv_b), w_ax, vec(b_a), vec(b_x), vec(lam))


def _mem_attn_kernel(q_ref, kv_ref, gate_ref, o_ref):
    scale = MEM_HEAD_DIM ** -0.5
    kv_width = MEM_HEADS * MEM_HEAD_DIM
    for hd in range(MEM_HEADS):
        cols = slice(hd * MEM_HEAD_DIM, (hd + 1) * MEM_HEAD_DIM)
        vcols = slice(kv_width + hd * MEM_HEAD_DIM, kv_width + (hd + 1) * MEM_HEAD_DIM)
        s = lax.dot_general(q_ref[:, cols], kv_ref[:, cols], (((1,), (1,)), ((), ())),
                            preferred_element_type=F32) * scale
        p = jnp.exp(s - jnp.max(s, axis=-1, keepdims=True))
        l = jnp.sum(p, axis=-1, keepdims=True)
        o = jnp.dot(p.astype(BF16), kv_ref[:, vcols], preferred_element_type=F32) / l
        o_ref[:, cols] = (o * _silu(gate_ref[:, cols].astype(F32))).astype(o_ref.dtype)


def _memory_attention(proj, mkv, *, seq, mem_len, col_q, col_gate, tq):
    n = proj.shape[0]
    width = MEM_HEADS * MEM_HEAD_DIM
    tiles_per_seq = seq // tq
    bq, bg = _col_block(col_q, width), _col_block(col_gate, width)
    vmem = 2 * (3 * tq * width * 2 + mem_len * 2 * width * 2) + 8 * tq * mem_len * 4 + (4 << 20)
    return pl.pallas_call(
        _mem_attn_kernel,
        grid=(n // tq,),
        in_specs=[
            pl.BlockSpec((tq, width), lambda i: (i, bq)),
            pl.BlockSpec((mem_len, 2 * width), lambda i: (i // tiles_per_seq, 0)),
            pl.BlockSpec((tq, width), lambda i: (i, bg)),
        ],
        out_specs=pl.BlockSpec((tq, width), lambda i: (i, 0)),
        out_shape=jax.ShapeDtypeStruct((n, width), BF16),
        compiler_params=_compiler_params(("arbitrary",), vmem),
        name="memory_attention",
    )(proj, mkv, proj)


def _merge_kernel(*refs):
    nb = (len(refs) - 1) // 3
    y_refs, w_refs, g_refs, o_ref = refs[:nb], refs[nb:2 * nb], refs[2 * nb:3 * nb], refs[-1]
    acc = None
    for y_ref, w_ref, g_ref in zip(y_refs, w_refs, g_refs):
        term = _sigmoid(g_ref[...].astype(F32)) * jnp.dot(y_ref[...], w_ref[...], preferred_element_type=F32)
        acc = term if acc is None else acc + term
    o_ref[...] = acc.astype(o_ref.dtype)


def _merge(branches, gates, w_branch, layer, *, col_merge, tm, tn):
    n = gates.shape[0]
    d = w_branch.shape[2]
    sizes = [y.shape[1] for y in branches]
    offsets = [sum(sizes[:i]) for i in range(len(sizes))]
    y_specs = [pl.BlockSpec((tm, sz), lambda i, j: (i, 0)) for sz in sizes]
    w_specs = [pl.BlockSpec((None, sz, tn), lambda i, j, r=_col_block(off, sz): (layer, r, j))
               for sz, off in zip(sizes, offsets)]
    g_specs = [pl.BlockSpec((tm, tn), lambda i, j, c=_col_block(col_merge + bi * d, tn): (i, c + j))
               for bi in range(len(sizes))]
    total = sum(sizes)
    vmem = 2 * (tm * total * 2 + total * tn * 2 + len(sizes) * tm * tn * 2 + tm * tn * 2) + 4 * tm * tn * 4 + (4 << 20)
    return pl.pallas_call(
        _merge_kernel,
        grid=(n // tm, d // tn),
        in_specs=y_specs + w_specs + g_specs,
        out_specs=pl.BlockSpec((tm, tn), lambda i, j: (i, j)),
        out_shape=jax.ShapeDtypeStruct((n, d), BF16),
        compiler_params=_compiler_params(("arbitrary", "arbitrary"), vmem),
        name="branch_merge",
    )(*branches, *([w_branch] * len(sizes)), *([gates] * len(sizes)))


def _out_proj_kernel(x_ref, m_ref, w_ref, g_ref, *o_refs, last):
    y = x_ref[...] + jnp.dot(m_ref[...], w_ref[...], preferred_element_type=F32)
    normed = _rms(y, g_ref[...])
    if last:
        o_refs[0][...] = normed
    else:
        o_refs[0][...] = y
        o_refs[1][...] = normed.astype(BF16)


def _out_proj(x, merged, w_out, layer, g, *, last, tm):
    n, d = x.shape
    vmem = 2 * (2 * tm * d * 4 + 2 * tm * d * 2 + d * d * 2) + 3 * tm * d * 4 + (4 << 20)
    row_spec = pl.BlockSpec((tm, d), lambda i: (i, 0))
    out_shape = [jax.ShapeDtypeStruct((n, d), F32)] + ([] if last else [jax.ShapeDtypeStruct((n, d), BF16)])
    return pl.pallas_call(
        functools.partial(_out_proj_kernel, last=last),
        grid=(n // tm,),
        in_specs=[row_spec, row_spec, pl.BlockSpec((None, d, d), lambda i: (layer, 0, 0)),
                  pl.BlockSpec((1, d), lambda i: (0, 0))],
        out_specs=[row_spec] * len(out_shape),
        out_shape=out_shape,
        compiler_params=_compiler_params(("arbitrary",), vmem),
        name="out_proj",
    )(x, merged, w_out, g.reshape(1, d))


def kernel(x, mem, norm_g, w_in, lam_vecs, subln_g, pool_w, pool_scale, conv_w, conv_b, lru_wa, lru_ba,
           lru_wx, lru_bx, lru_lambda, mem_norm_g, w_mem_kv, w_branch, w_out, final_g):
    batch, seq, d = x.shape
    mem_len = mem.shape[1]
    depth = w_in.shape[0]
    da_width = DA_HEADS * DA_V_DIM
    pool_width = pool_scale.shape[-1]
    lru_width = conv_b.shape[-1]
    mem_width = MEM_HEADS * MEM_HEAD_DIM
    branch_width = da_width + pool_width + lru_width + mem_width
    col_q = 0
    col_k = col_q + da_width
    col_v = col_k + da_width
    col_pool = col_v + da_width
    col_lru = col_pool + pool_width
    col_qmem = col_lru + lru_width
    col_gate = col_qmem + mem_width
    col_merge = col_gate + branch_width
    gate_a = col_gate
    gate_b = gate_a + da_width
    gate_c = gate_b + pool_width
    gate_m = gate_c + lru_width

    xf = x.reshape(batch * seq, d)
    memf = mem.reshape(batch * mem_len, d)
    w_branch_bf16 = w_branch.astype(BF16)
    w_out_bf16 = w_out.astype(BF16)
    h = _norm(xf, norm_g[0], tm=1024)
    for l in range(depth):
        last = l == depth - 1
        lambda_init = 0.8 - 0.6 * math.exp(-0.3 * l)
        proj = _proj(h, w_in, l, panels=_col_block(col_merge, PROJ_TN), tm=2048, tn=PROJ_TN, name="in_proj")
        w_ax = jnp.concatenate([lru_wa[l], lru_wx[l]], axis=-1).astype(BF16)
        merge_gates, y_c = _proj_lru(h, w_in, l, proj, conv_w[l], conv_b[l], w_ax, lru_ba[l], lru_bx[l],
                                     lru_lambda[l], seq=seq, panel0=_col_block(col_merge, PROJ_TN),
                                     col_x=col_lru, col_gate=gate_c, tm=1024, tn=PROJ_TN)
        mkv = _norm_proj(memf, mem_norm_g[l], w_mem_kv[l].astype(BF16), tm=batch * mem_len, tn=512, name="mem_kv")
        y_a = _diff_attention(proj, lam_vecs[l], subln_g[l], batch=batch, seq=seq, col_q=col_q, col_k=col_k,
                              col_v=col_v, col_gate=gate_a, lambda_init=lambda_init, tq=512)
        y_b = _multiscale_pool(proj, pool_w[l].astype(BF16), pool_scale[l], seq=seq, col_x=col_pool,
                               col_gate=gate_b, ts=1024)
        y_m = _memory_attention(proj, mkv, seq=seq, mem_len=mem_len, col_q=col_qmem, col_gate=gate_m, tq=1024)
        merged = _merge([y_a, y_b, y_c, y_m], merge_gates, w_branch_bf16, l, col_merge=0, tm=1024, tn=512)
        if last:
            (out,) = _out_proj(xf, merged, w_out_bf16, l, final_g, last=True, tm=512)
        else:
            xf, h = _out_proj(xf, merged, w_out_bf16, l, norm_g[l + 1], last=False, tm=512)
    return out.reshape(batch, seq, d)
```

```python
import functools
import math

import jax
import jax.numpy as jnp
from jax import lax
from jax.experimental import pallas as pl
from jax.experimental.pallas import tpu as pltpu

F32 = jnp.float32
BF16 = jnp.bfloat16

EPS = 1e-6
DA_HEADS = 8
DA_HEAD_DIM = 64
DA_V_DIM = 2 * DA_HEAD_DIM
POOL_WINDOWS = (2, 4, 8, 16)
POOL_HALO = 32
LRU_BLOCKS = 8
CONV_WIDTH = 4
CONV_HALO = 8
LRU_C = 8.0
MEM_HEADS = 4
MEM_HEAD_DIM = 128
LANES = 128
SUBLANES = 8
V7X_VMEM_BYTES = 64 * 1024 * 1024
PROJ_TN = 1024
MASK_VALUE = -1e30
LOG2E = 1.4426950408889634


def _compiler_params(semantics, vmem_bytes):
    assert vmem_bytes < V7X_VMEM_BYTES
    return pltpu.CompilerParams(dimension_semantics=semantics, vmem_limit_bytes=int(vmem_bytes))


def _col_block(col, width):
    assert col % width == 0, (col, width)
    return col // width


def _rms(x, g):
    ms = jnp.mean(x * x, axis=-1, keepdims=True)
    return x * lax.rsqrt(ms + EPS) * g


def _segment_pitch(seg):
    return seg + (4 - seg) % 8


def _lane_tile(x, n):
    return jnp.concatenate([x] * n, axis=1)


def _sigmoid(x):
    return 0.5 * jnp.tanh(0.5 * x) + 0.5


def _silu(x):
    return x * _sigmoid(x)


def _norm_proj_kernel(x_ref, g_ref, w_ref, o_ref, h_ref):
    @pl.when(pl.program_id(1) == 0)
    def _():
        h_ref[...] = _rms(x_ref[...], g_ref[...]).astype(BF16)

    o_ref[...] = jnp.dot(h_ref[...], w_ref[...], preferred_element_type=F32).astype(o_ref.dtype)


def _norm_proj(x, g, w, *, tm, tn, name):
    n, d = x.shape
    nc = w.shape[1]
    vmem = 2 * (tm * d * 4 + d * tn * 2 + tm * tn * 2) + tm * d * 2 + (4 << 20)
    return pl.pallas_call(
        _norm_proj_kernel,
        grid=(n // tm, nc // tn),
        in_specs=[
            pl.BlockSpec((tm, d), lambda i, j: (i, 0)),
            pl.BlockSpec((1, d), lambda i, j: (0, 0)),
            pl.BlockSpec((d, tn), lambda i, j: (0, j)),
        ],
        out_specs=pl.BlockSpec((tm, tn), lambda i, j: (i, j)),
        out_shape=jax.ShapeDtypeStruct((n, nc), BF16),
        scratch_shapes=[pltpu.VMEM((tm, d), BF16)],
        compiler_params=_compiler_params(("arbitrary", "arbitrary"), vmem),
        name=name,
    )(x, g.reshape(1, d), w)


def _norm_kernel(x_ref, g_ref, o_ref):
    o_ref[...] = _rms(x_ref[...], g_ref[...]).astype(o_ref.dtype)


def _norm(x, g, *, tm):
    n, d = x.shape
    vmem = 2 * (tm * d * 4 + tm * d * 2) + 3 * tm * d * 4 + (4 << 20)
    return pl.pallas_call(
        _norm_kernel,
        grid=(n // tm,),
        in_specs=[pl.BlockSpec((tm, d), lambda i: (i, 0)), pl.BlockSpec((1, d), lambda i: (0, 0))],
        out_specs=pl.BlockSpec((tm, d), lambda i: (i, 0)),
        out_shape=jax.ShapeDtypeStruct((n, d), BF16),
        compiler_params=_compiler_params(("arbitrary",), vmem),
        name="rms_norm",
    )(x, g.reshape(1, d))


def _proj_kernel(h_ref, w_ref, o_ref, wb_ref):
    @pl.when(pl.program_id(1) == 0)
    def _():
        wb_ref[...] = w_ref[...].astype(BF16)

    o_ref[...] = jnp.dot(h_ref[...], wb_ref[...], preferred_element_type=F32).astype(o_ref.dtype)


def _proj(h, w_stack, layer, *, panels, tm, tn, name):
    n, d = h.shape
    vmem = 2 * (tm * d * 2 + d * tn * 4 + tm * tn * 2) + d * tn * 2 + tm * tn * 4 + (4 << 20)
    return pl.pallas_call(
        _proj_kernel,
        grid=(panels, n // tm),
        in_specs=[
            pl.BlockSpec((tm, d), lambda j, i: (i, 0)),
            pl.BlockSpec((None, d, tn), lambda j, i: (layer, 0, j)),
        ],
        out_specs=pl.BlockSpec((tm, tn), lambda j, i: (i, j)),
        out_shape=jax.ShapeDtypeStruct((n, panels * tn), BF16),
        scratch_shapes=[pltpu.VMEM((d, tn), BF16)],
        compiler_params=_compiler_params(("arbitrary", "arbitrary"), vmem),
        name=name,
    )(h, w_stack)


def _diff_attn_kernel(q_ref, k_ref, v_ref, gate_ref, lamv_ref, subg_ref, o_ref,
                      qs_ref, vx_ref, m_ref, acc_ref, sa_ref, sb_ref, *, tq, lambda_init):
    g = pl.program_id(2)
    rows = 2 * tq
    hw = DA_V_DIM
    scale2 = (DA_HEAD_DIM ** -0.5) * LOG2E

    @pl.when(g == 0)
    def _():
        vx_ref[:, 0:hw] = v_ref[...]
        vx_ref[:, hw:2 * hw] = jnp.ones(v_ref.shape, v_ref.dtype)

    for t in range(2):
        q = q_ref[t * tq:(t + 1) * tq, :]
        lane = lax.broadcasted_iota(jnp.int32, q.shape, 1)
        zero = jnp.zeros_like(q)
        qs_ref[t * rows:t * rows + tq, :] = jnp.where(lane < DA_HEAD_DIM, q, zero)
        qs_ref[t * rows + tq:(t + 1) * rows, :] = jnp.where(lane >= DA_HEAD_DIM, q, zero)
    m_ref[...] = jnp.full(m_ref.shape, MASK_VALUE, F32)
    acc_ref[...] = jnp.zeros(acc_ref.shape, F32)

    def scores(j, s_ref, row0, nrows):
        k = k_ref[pl.ds(pl.multiple_of(j * tq, tq), tq), :]
        s_ref[0:nrows, :] = lax.dot_general(qs_ref[row0:row0 + nrows, :], k, (((1,), (1,)), ((), ())),
                                            preferred_element_type=F32)

    def update(j, s_ref, row0, nrows, diag_tile):
        s = s_ref[0:nrows, :]
        if diag_tile is not None:
            assert row0 == diag_tile * rows
            causal = (lax.broadcasted_iota(jnp.int32, (tq, tq), 1)
                      <= lax.broadcasted_iota(jnp.int32, (tq, tq), 0))
            pieces = [jnp.where(causal, s[k * tq:(k + 1) * tq, :], MASK_VALUE) for k in range(2)]
            s = jnp.concatenate(pieces + ([s[rows:nrows, :]] if nrows > rows else []), axis=0)
        m_prev = m_ref[row0:row0 + nrows, :]
        m_next = jnp.maximum(m_prev, jnp.max(s, axis=-1, keepdims=True))
        p = jnp.exp2((s - _lane_tile(m_next, tq // LANES)) * scale2)
        alpha = jnp.exp2((m_prev - m_next) * scale2)
        vx = vx_ref[pl.ds(pl.multiple_of(j * tq, tq), tq), :]
        pv = jnp.dot(p.astype(BF16), vx, preferred_element_type=F32)
        acc_ref[row0:row0 + nrows, :] = _lane_tile(alpha, 2) * acc_ref[row0:row0 + nrows, :] + pv
        m_ref[row0:row0 + nrows, :] = m_next

    lv = lamv_ref[...]
    lam = (jnp.exp(jnp.sum(lv[0:1, :] * lv[1:2, :], axis=-1, keepdims=True))
           - jnp.exp(jnp.sum(lv[2:3, :] * lv[3:4, :], axis=-1, keepdims=True)) + lambda_init)

    def finalize(t):
        o_maps = acc_ref[t * rows:(t + 1) * rows, 0:hw] / acc_ref[t * rows:(t + 1) * rows, hw:2 * hw]
        o = o_maps[0:tq, :] - lam * o_maps[tq:rows, :]
        y = _rms(o, subg_ref[...]) * (1.0 - lambda_init)
        out_rows = slice(t * tq, (t + 1) * tq)
        o_ref[out_rows, :] = (y * _silu(gate_ref[out_rows, :].astype(F32))).astype(o_ref.dtype)

    scores(0, sa_ref, 0, 2 * rows)

    def unmasked_pair(jj, carry):
        scores(2 * jj + 1, sb_ref, 0, 2 * rows)
        update(2 * jj, sa_ref, 0, 2 * rows, None)
        scores(2 * jj + 2, sa_ref, 0, 2 * rows)
        update(2 * jj + 1, sb_ref, 0, 2 * rows, None)
        return carry

    lax.fori_loop(0, g, unmasked_pair, 0)
    scores(2 * g + 1, sb_ref, rows, rows)
    update(2 * g, sa_ref, 0, 2 * rows, 0)
    finalize(0)
    update(2 * g + 1, sb_ref, rows, rows, 1)
    finalize(1)


def _diff_attention(proj, lam_vecs, subln_g, *, batch, seq, col_q, col_k, col_v, col_gate, lambda_init, tq):
    n = batch * seq
    assert seq % (2 * tq) == 0
    ng = seq // (2 * tq)
    hw = DA_V_DIM
    bq, bk, bv, bg = (_col_block(c, hw) for c in (col_q, col_k, col_v, col_gate))
    vmem = (2 * (3 * 2 * tq * hw * 2 + 2 * seq * hw * 2) + 2 * 2 * tq * hw * (2 + 12) + seq * 2 * hw * 2
            + 6 * 4 * tq * tq * 4 + (4 << 20))
    kernel = functools.partial(_diff_attn_kernel, tq=tq, lambda_init=lambda_init)
    return pl.pallas_call(
        kernel,
        grid=(batch, DA_HEADS, ng),
        in_specs=[
            pl.BlockSpec((2 * tq, hw), lambda b, h, i: (b * ng + i, bq + h)),
            pl.BlockSpec((seq, hw), lambda b, h, i: (b, bk + h)),
            pl.BlockSpec((seq, hw), lambda b, h, i: (b, bv + h)),
            pl.BlockSpec((2 * tq, hw), lambda b, h, i: (b * ng + i, bg + h)),
            pl.BlockSpec(lam_vecs.shape, lambda b, h, i: (0, 0)),
            pl.BlockSpec((1, hw), lambda b, h, i: (0, 0)),
        ],
        out_specs=pl.BlockSpec((2 * tq, hw), lambda b, h, i: (b * ng + i, h)),
        out_shape=jax.ShapeDtypeStruct((n, DA_HEADS * hw), BF16),
        scratch_shapes=[
            pltpu.VMEM((4 * tq, hw), BF16),
            pltpu.VMEM((seq, 2 * hw), BF16),
            pltpu.VMEM((4 * tq, LANES), F32),
            pltpu.VMEM((4 * tq, 2 * hw), F32),
            pltpu.VMEM((4 * tq, tq), F32),
            pltpu.VMEM((4 * tq, tq), F32),
        ],
        compiler_params=_compiler_params(("arbitrary", "arbitrary", "arbitrary"), vmem),
        name="diff_attention",
    )(proj, proj, proj, proj, lam_vecs, subln_g.reshape(1, hw))


def _gate_cols(gate_refs, cols):
    piece = gate_refs[0].shape[1]
    idx, start = divmod(cols.start, piece)
    assert cols.stop - cols.start <= piece - start
    return _silu(gate_refs[idx][:, start:start + cols.stop - cols.start].astype(F32))


def _pool_kernel(x_ref, halo_ref, gate_lo_ref, gate_hi_ref, w_ref, scale_ref, o_ref, xe_ref, la_ref, lb_ref,
                 *, ts, tiles_per_seq):
    gate_refs = (gate_lo_ref, gate_hi_ref)
    t = pl.program_id(0) % tiles_per_seq
    halo = halo_ref[...].astype(F32)
    xe_ref[0:POOL_HALO, :] = jnp.where(t == 0, jnp.zeros_like(halo), halo)
    x = x_ref[...].astype(F32)
    xe_ref[POOL_HALO:POOL_HALO + ts, :] = x
    gw = x.shape[1] // len(POOL_WINDOWS)
    pos1 = t * ts + lax.broadcasted_iota(jnp.int32, (ts, gw), 0) + 1
    for gi, w in enumerate(POOL_WINDOWS):
        cols = slice(gi * gw, (gi + 1) * gw)
        xg = x[:, cols]
        levels = w.bit_length() - 1
        src_ref, src_cols, spare = xe_ref, cols, [la_ref, lb_ref]
        for lv in range(levels):
            shift = 1 << lv
            lo = POOL_HALO if lv == levels - 1 else SUBLANES * (lv + 1)
            nrows = POOL_HALO + ts - lo
            wsum = src_ref[lo:lo + nrows, src_cols] + src_ref[lo - shift:lo - shift + nrows, src_cols]
            if lv < levels - 1:
                dst_ref = spare[lv % 2]
                dst_ref[lo:lo + nrows, :] = wsum
                src_ref, src_cols = dst_ref, slice(0, gw)
        count = jnp.minimum(pos1, w).astype(F32)
        d = (wsum / count - xg).astype(BF16)
        y = jnp.dot(d, w_ref[gi], preferred_element_type=F32) * scale_ref[:, cols]
        o_ref[:, cols] = (y * _gate_cols(gate_refs, cols)).astype(o_ref.dtype)


def _multiscale_pool(proj, pool_w, pool_scale, *, seq, col_x, col_gate, ts):
    n = proj.shape[0]
    width = pool_scale.shape[-1]
    tiles_per_seq = seq // ts
    halo_per_tile = ts // POOL_HALO
    half = width // 2
    bx, bg = _col_block(col_x, width), _col_block(col_gate, half)
    vmem = 2 * (3 * ts * width * 2 + POOL_HALO * width * 2 + pool_w.size * 2) + 8 * ts * width * 4 + (4 << 20)
    kernel = functools.partial(_pool_kernel, ts=ts, tiles_per_seq=tiles_per_seq)
    return pl.pallas_call(
        kernel,
        grid=(n // ts,),
        in_specs=[
            pl.BlockSpec((ts, width), lambda i: (i, bx)),
            pl.BlockSpec((POOL_HALO, width), lambda i: (jnp.maximum(i * halo_per_tile - 1, 0), bx)),
            pl.BlockSpec((ts, half), lambda i: (i, bg)),
            pl.BlockSpec((ts, half), lambda i: (i, bg + 1)),
            pl.BlockSpec(pool_w.shape, lambda i: (0, 0, 0)),
            pl.BlockSpec((1, width), lambda i: (0, 0)),
        ],
        out_specs=pl.BlockSpec((ts, width), lambda i: (i, 0)),
        out_shape=jax.ShapeDtypeStruct((n, width), BF16),
        scratch_shapes=[pltpu.VMEM((POOL_HALO + ts, width), F32)]
        + [pltpu.VMEM((POOL_HALO + ts, width // len(POOL_WINDOWS)), F32)] * 2,
        compiler_params=_compiler_params(("arbitrary",), vmem),
        name="multiscale_pool",
    )(proj, proj, proj, proj, pool_w, pool_scale.reshape(1, width))


def _proj_lru_kernel(h_ref, w_ref, x_ref, gate_ref, cw_ref, cb_ref, wax_ref, ba_ref, bx_ref, lam_ref,
                     o_ref, y_ref, wb_ref, xe_ref, carry_ref, a_ref, u_ref, *, tiles_per_seq):
    i = pl.program_id(1)
    t = i % tiles_per_seq
    ts, bw = x_ref.shape

    @pl.when(i == 0)
    def _():
        wb_ref[...] = w_ref[...].astype(BF16)

    @pl.when(t == 0)
    def _():
        xe_ref[0:CONV_HALO, :] = jnp.zeros((CONV_HALO, bw), F32)
        carry_ref[...] = jnp.zeros(carry_ref.shape, F32)

    @pl.when(t > 0)
    def _():
        xe_ref[0:CONV_HALO, :] = xe_ref[ts:ts + CONV_HALO, :]

    xe_ref[CONV_HALO:CONV_HALO + ts, :] = x_ref[...].astype(F32)

    xc = cb_ref[...] + cw_ref[CONV_WIDTH - 1:CONV_WIDTH, :] * xe_ref[CONV_HALO:CONV_HALO + ts, :]
    for back in range(1, CONV_WIDTH):
        tap = CONV_WIDTH - 1 - back
        xc = xc + cw_ref[tap:tap + 1, :] * xe_ref[CONV_HALO - back:CONV_HALO - back + ts, :]

    neg_lam = -lam_ref[...]
    softplus = jnp.maximum(neg_lam, 0.0) + jnp.log(1.0 + jnp.exp(-jnp.abs(neg_lam)))
    z = jnp.dot(xc.astype(BF16), wax_ref[...], preferred_element_type=F32)
    o_ref[...] = jnp.dot(h_ref[...], wb_ref[...], preferred_element_type=F32).astype(o_ref.dtype)

    r = _sigmoid(z[:, 0:bw] + ba_ref[...])
    gate_i = _sigmoid(z[:, bw:2 * bw] + bx_ref[...])
    log_a = (-LRU_C) * r * softplus
    th = jnp.tanh(log_a)
    m2 = -2.0 * th / (1.0 - th)
    mult = jnp.where(m2 > 0.0, m2 * lax.rsqrt(m2), 0.0)
    row = lax.broadcasted_iota(jnp.int32, (ts, bw), 0)
    first_row = jnp.where(t == 0, 0, -1)
    mult = jnp.where(row == first_row, 1.0, mult)
    a_all = jnp.exp(log_a)
    u_all = mult * (gate_i * xc)

    seg = ts // SUBLANES
    pitch = _segment_pitch(seg)
    for i in range(SUBLANES):
        a_ref[i * pitch:i * pitch + seg, :] = a_all[i * seg:(i + 1) * seg, :]
        u_ref[i * pitch:i * pitch + seg, :] = u_all[i * seg:(i + 1) * seg, :]
    step = lambda ref, j: ref[pl.ds(j, SUBLANES, stride=pitch), :]
    h_end = jnp.zeros((SUBLANES, bw), F32)
    a_end = jnp.ones((SUBLANES, bw), F32)
    for j in range(seg):
        a_j = step(a_ref, j)
        h_end = a_j * h_end + step(u_ref, j)
        a_end = a_j * a_end
    sub = lax.broadcasted_iota(jnp.int32, (SUBLANES, bw), 0)
    for k in (1, 2, 4):
        keep = sub >= k
        a_prev = jnp.where(keep, pltpu.roll(a_end, k, 0), 1.0)
        h_prev = jnp.where(keep, pltpu.roll(h_end, k, 0), 0.0)
        h_end = a_end * h_prev + h_end
        a_end = a_end * a_prev
    carry_in = carry_ref[...]
    state = a_end * carry_in + h_end
    carry_ref[...] = jnp.broadcast_to(state[SUBLANES - 1:SUBLANES, :], state.shape)
    h = jnp.where(sub >= 1, pltpu.roll(state, 1, 0), carry_in)
    for j in range(seg):
        h = step(a_ref, j) * h + step(u_ref, j)
        u_ref[pl.ds(j, SUBLANES, stride=pitch), :] = h
    hs = jnp.concatenate([u_ref[i * pitch:i * pitch + seg, :] for i in range(SUBLANES)], axis=0)
    y_ref[...] = (hs * _silu(gate_ref[...].astype(F32))).astype(y_ref.dtype)


def _proj_lru(h, w_stack, layer, proj_a, conv_w, conv_b, w_ax, b_a, b_x, lam, *, seq, panel0, col_x, col_gate,
              tm, tn):
    n, d = h.shape
    assert w_stack.shape[2] == (panel0 + LRU_BLOCKS) * tn and seq % tm == 0
    width = conv_b.shape[-1]
    bw = width // LRU_BLOCKS
    bx, bg = _col_block(col_x, bw), _col_block(col_gate, bw)
    vec = lambda a: a.reshape(1, width)
    vmem = (2 * (tm * d * 2 + d * tn * 4 + tm * tn * 2) + d * tn * 2 + tm * tn * 4
            + 24 * tm * bw * 4 + (4 << 20))
    kernel = functools.partial(_proj_lru_kernel, tiles_per_seq=seq // tm)
    vec_spec = pl.BlockSpec((1, bw), lambda p, i: (0, p))
    return pl.pallas_call(
        kernel,
        grid=(LRU_BLOCKS, n // tm),
        in_specs=[
            pl.BlockSpec((tm, d), lambda p, i: (i, 0)),
            pl.BlockSpec((None, d, tn), lambda p, i: (layer, 0, panel0 + p)),
            pl.BlockSpec((tm, bw), lambda p, i: (i, bx + p)),
            pl.BlockSpec((tm, bw), lambda p, i: (i, bg + p)),
            pl.BlockSpec((CONV_WIDTH, bw), lambda p, i: (0, p)),
            vec_spec,
            pl.BlockSpec((None, bw, 2 * bw), lambda p, i: (p, 0, 0)),
            vec_spec, vec_spec, vec_spec,
        ],
        out_specs=[
            pl.BlockSpec((tm, tn), lambda p, i: (i, p)),
            pl.BlockSpec((tm, bw), lambda p, i: (i, p)),
        ],
        out_shape=[jax.ShapeDtypeStruct((n, LRU_BLOCKS * tn), BF16), jax.ShapeDtypeStruct((n, width), BF16)],
        scratch_shapes=[
            pltpu.VMEM((d, tn), BF16),
            pltpu.VMEM((CONV_HALO + tm, bw), F32),
            pltpu.VMEM((SUBLANES, bw), F32),
            pltpu.VMEM((SUBLANES * _segment_pitch(tm // SUBLANES), bw), F32),
            pltpu.VMEM((SUBLANES * _segment_pitch(tm // SUBLANES), bw), F32),
        ],
        compiler_params=_compiler_params(("arbitrary", "arbitrary"), vmem),
        name="in_proj_lru",
    )(h, w_stack, proj_a, proj_a, conv_w, vec(conv_b), w_ax, vec(b_a), vec(b_x), vec(lam))


def _mem_attn_kernel(q_ref, kv_ref, gate_ref, o_ref):
    scale = MEM_HEAD_DIM ** -0.5
    kv_width = MEM_HEADS * MEM_HEAD_DIM
    for hd in range(MEM_HEADS):
        cols = slice(hd * MEM_HEAD_DIM, (hd + 1) * MEM_HEAD_DIM)
        vcols = slice(kv_width + hd * MEM_HEAD_DIM, kv_width + (hd + 1) * MEM_HEAD_DIM)
        s = lax.dot_general(q_ref[:, cols], kv_ref[:, cols], (((1,), (1,)), ((), ())),
                            preferred_element_type=F32) * scale
        p = jnp.exp(s - jnp.max(s, axis=-1, keepdims=True))
        l = jnp.sum(p, axis=-1, keepdims=True)
        o = jnp.dot(p.astype(BF16), kv_ref[:, vcols], preferred_element_type=F32) / l
        o_ref[:, cols] = (o * _silu(gate_ref[:, cols].astype(F32))).astype(o_ref.dtype)


def _memory_attention(proj, mkv, *, seq, mem_len, col_q, col_gate, tq):
    n = proj.shape[0]
    width = MEM_HEADS * MEM_HEAD_DIM
    tiles_per_seq = seq // tq
    bq, bg = _col_block(col_q, width), _col_block(col_gate, width)
    vmem = 2 * (3 * tq * width * 2 + mem_len * 2 * width * 2) + 8 * tq * mem_len * 4 + (4 << 20)
    return pl.pallas_call(
        _mem_attn_kernel,
        grid=(n // tq,),
        in_specs=[
            pl.BlockSpec((tq, width), lambda i: (i, bq)),
            pl.BlockSpec((mem_len, 2 * width), lambda i: (i // tiles_per_seq, 0)),
            pl.BlockSpec((tq, width), lambda i: (i, bg)),
        ],
        out_specs=pl.BlockSpec((tq, width), lambda i: (i, 0)),
        out_shape=jax.ShapeDtypeStruct((n, width), BF16),
        compiler_params=_compiler_params(("arbitrary",), vmem),
        name="memory_attention",
    )(proj, mkv, proj)


def _merge_kernel(*refs):
    nb = (len(refs) - 1) // 3
    y_refs, w_refs, g_refs, o_ref = refs[:nb], refs[nb:2 * nb], refs[2 * nb:3 * nb], refs[-1]
    acc = None
    for y_ref, w_ref, g_ref in zip(y_refs, w_refs, g_refs):
        term = _sigmoid(g_ref[...].astype(F32)) * jnp.dot(y_ref[...], w_ref[...], preferred_element_type=F32)
        acc = term if acc is None else acc + term
    o_ref[...] = acc.astype(o_ref.dtype)


def _merge(branches, gates, w_branch, layer, *, col_merge, tm, tn):
    n = gates.shape[0]
    d = w_branch.shape[2]
    sizes = [y.shape[1] for y in branches]
    offsets = [sum(sizes[:i]) for i in range(len(sizes))]
    y_specs = [pl.BlockSpec((tm, sz), lambda j, i: (i, 0)) for sz in sizes]
    w_specs = [pl.BlockSpec((None, sz, tn), lambda j, i, r=_col_block(off, sz): (layer, r, j))
               for sz, off in zip(sizes, offsets)]
    g_specs = [pl.BlockSpec((tm, tn), lambda j, i, c=_col_block(col_merge + bi * d, tn): (i, c + j))
               for bi in range(len(sizes))]
    total = sum(sizes)
    vmem = 2 * (tm * total * 2 + total * tn * 2 + len(sizes) * tm * tn * 2 + tm * tn * 2) + 4 * tm * tn * 4 + (4 << 20)
    return pl.pallas_call(
        _merge_kernel,
        grid=(d // tn, n // tm),
        in_specs=y_specs + w_specs + g_specs,
        out_specs=pl.BlockSpec((tm, tn), lambda j, i: (i, j)),
        out_shape=jax.ShapeDtypeStruct((n, d), BF16),
        compiler_params=_compiler_params(("arbitrary", "arbitrary"), vmem),
        name="branch_merge",
    )(*branches, *([w_branch] * len(sizes)), *([gates] * len(sizes)))


def _out_proj_kernel(x_ref, m_ref, w_ref, g_ref, *o_refs, last):
    y = x_ref[...] + jnp.dot(m_ref[...], w_ref[...], preferred_element_type=F32)
    normed = _rms(y, g_ref[...])
    if last:
        o_refs[0][...] = normed
    else:
        o_refs[0][...] = y
        o_refs[1][...] = normed.astype(BF16)


def _out_proj(x, merged, w_out, layer, g, *, last, tm):
    n, d = x.shape
    vmem = 2 * (2 * tm * d * 4 + 2 * tm * d * 2 + d * d * 2) + 3 * tm * d * 4 + (4 << 20)
    row_spec = pl.BlockSpec((tm, d), lambda i: (i, 0))
    out_shape = [jax.ShapeDtypeStruct((n, d), F32)] + ([] if last else [jax.ShapeDtypeStruct((n, d), BF16)])
    return pl.pallas_call(
        functools.partial(_out_proj_kernel, last=last),
        grid=(n // tm,),
        in_specs=[row_spec, row_spec, pl.BlockSpec((None, d, d), lambda i: (layer, 0, 0)),
                  pl.BlockSpec((1, d), lambda i: (0, 0))],
        out_specs=[row_spec] * len(out_shape),
        out_shape=out_shape,
        compiler_params=_compiler_params(("arbitrary",), vmem),
        name="out_proj",
    )(x, merged, w_out, g.reshape(1, d))


def kernel(x, mem, norm_g, w_in, lam_vecs, subln_g, pool_w, pool_scale, conv_w, conv_b, lru_wa, lru_ba,
           lru_wx, lru_bx, lru_lambda, mem_norm_g, w_mem_kv, w_branch, w_out, final_g):
    batch, seq, d = x.shape
    mem_len = mem.shape[1]
    depth = w_in.shape[0]
    da_width = DA_HEADS * DA_V_DIM
    pool_width = pool_scale.shape[-1]
    lru_width = conv_b.shape[-1]
    mem_width = MEM_HEADS * MEM_HEAD_DIM
    branch_width = da_width + pool_width + lru_width + mem_width
    col_q = 0
    col_k = col_q + da_width
    col_v = col_k + da_width
    col_pool = col_v + da_width
    col_lru = col_pool + pool_width
    col_qmem = col_lru + lru_width
    col_gate = col_qmem + mem_width
    col_merge = col_gate + branch_width
    gate_a = col_gate
    gate_b = gate_a + da_width
    gate_c = gate_b + pool_width
    gate_m = gate_c + lru_width

    xf = x.reshape(batch * seq, d)
    memf = mem.reshape(batch * mem_len, d)
    w_branch_bf16 = w_branch.astype(BF16)
    w_out_bf16 = w_out.astype(BF16)
    h = _norm(xf, norm_g[0], tm=1024)
    for l in range(depth):
        last = l == depth - 1
        lambda_init = 0.8 - 0.6 * math.exp(-0.3 * l)
        proj = _proj(h, w_in, l, panels=_col_block(col_merge, PROJ_TN), tm=2048, tn=PROJ_TN, name="in_proj")
        w_ax = jnp.concatenate([lru_wa[l], lru_wx[l]], axis=-1).astype(BF16)
        merge_gates, y_c = _proj_lru(h, w_in, l, proj, conv_w[l], conv_b[l], w_ax, lru_ba[l], lru_bx[l],
                                     lru_lambda[l], seq=seq, panel0=_col_block(col_merge, PROJ_TN),
                                     col_x=col_lru, col_gate=gate_c, tm=1024, tn=PROJ_TN)
        mkv = _norm_proj(memf, mem_norm_g[l], w_mem_kv[l].astype(BF16), tm=batch * mem_len, tn=512, name="mem_kv")
        y_a = _diff_attention(proj, lam_vecs[l], subln_g[l], batch=batch, seq=seq, col_q=col_q, col_k=col_k,
                              col_v=col_v, col_gate=gate_a, lambda_init=lambda_init, tq=512)
        y_b = _multiscale_pool(proj, pool_w[l].astype(BF16), pool_scale[l], seq=seq, col_x=col_pool,
                               col_gate=gate_b, ts=1024)
        y_m = _memory_attention(proj, mkv, seq=seq, mem_len=mem_len, col_q=col_qmem, col_gate=gate_m, tq=1024)
        merged = _merge([y_a, y_b, y_c, y_m], merge_gates, w_branch_bf16, l, col_merge=0, tm=1024, tn=512)
        if last:
            (out,) = _out_proj(xf, merged, w_out_bf16, l, final_g, last=True, tm=512)
        else:
            xf, h = _out_proj(xf, merged, w_out_bf16, l, norm_g[l + 1], last=False, tm=512)
    return out.reshape(batch, seq, d)
```
